```python
import jax, jax.numpy as jnp
from jax import lax
import numpy as np

D_MODEL = 1024
BATCH = 8
SEQ = 4096
DEPTH = 1

SB_HEADS = 8
SB_HEAD_DIM = 64
SB_WIDTH = SB_HEADS * SB_HEAD_DIM
Q_BLOCK = 128
HG_HEADS = 4
HG_KEY_DIM = 128
HG_VAL_DIM = 128
HG_KEY_WIDTH = HG_HEADS * HG_KEY_DIM
HG_WIDTH = HG_HEADS * HG_VAL_DIM
HG_CHUNK = 64
MIX_WIDTH = SB_WIDTH + HG_WIDTH
FFN_HIDDEN = -(-8 * D_MODEL // (3 * 256)) * 256
IN_SIZES = (SB_WIDTH, SB_WIDTH, SB_WIDTH, HG_KEY_WIDTH, HG_KEY_WIDTH,
            HG_WIDTH, HG_WIDTH, D_MODEL, D_MODEL)
IN_WIDTH = sum(IN_SIZES)
EPS = 1e-6

kernel_name = "stickbreaking_hgrn2_gated_hybrid_block"


def rms_norm(x, g):
    xf = x.astype(jnp.float32)
    y = xf * lax.rsqrt(jnp.mean(xf * xf, axis=-1, keepdims=True) + EPS)
    return (y * g.astype(jnp.float32)).astype(x.dtype)


def split_cols(t, sizes):
    idx, acc = [], 0
    for s in sizes[:-1]:
        acc += s
        idx.append(acc)
    return jnp.split(t, idx, axis=-1)


def stick_breaking_attention(q, k, v):
    seq = q.shape[2]
    scale = SB_HEAD_DIM ** -0.5
    outs = []
    for blk in range(seq // Q_BLOCK):
        t0 = blk * Q_BLOCK
        end = t0 + Q_BLOCK
        qb, kb, vb = q[:, :, t0:end], k[:, :, :end], v[:, :, :end]
        z = jnp.einsum('bhtd,bhsd->bhts', qb, kb).astype(jnp.float32) * scale
        t_idx = t0 + jnp.arange(Q_BLOCK)[:, None]
        s_idx = jnp.arange(end)[None, :]
        causal = s_idx < t_idx
        sp = jnp.where(causal, jax.nn.softplus(z), 0.0)
        rest = lax.cumsum(sp, axis=3, reverse=True) - sp
        w = jnp.where(causal, jnp.exp(jax.nn.log_sigmoid(z) - rest), 0.0)
        outs.append(jnp.einsum('bhts,bhsd->bhtd', w.astype(vb.dtype), vb))
    return jnp.concatenate(outs, axis=2)


def hgrn2_mixer(q_in, f_in, i_in, g_in, lb, norm_g):
    B, S, _ = q_in.shape
    n_chunks = S // HG_CHUNK
    f = lb + (1.0 - lb) * jax.nn.sigmoid(f_in.astype(jnp.float32))
    log_f = jnp.log(f)
    key = 1.0 - f
    q = jax.nn.silu(q_in.astype(jnp.float32))
    val = i_in.astype(jnp.float32)

    def to_chunks(t):
        return t.reshape(B, n_chunks, HG_CHUNK, HG_HEADS, -1).transpose(1, 0, 3, 2, 4)

    mask = jnp.tril(jnp.ones((HG_CHUNK, HG_CHUNK), dtype=bool))[:, :, None]

    def step(state, xs):
        qc, kc, vc, gc = xs
        b = jnp.cumsum(gc, axis=2)
        diff = b[:, :, :, None, :] - b[:, :, None, :, :]
        decay = jnp.exp(jnp.where(mask, diff, -jnp.inf))
        attn = jnp.einsum('bhtk,bhsk,bhtsk->bhts', qc, kc, decay)
        o_intra = jnp.einsum('bhts,bhsv->bhtv', attn, vc)
        o_inter = jnp.einsum('bhtk,bhkv->bhtv', qc * jnp.exp(b), state)
        b_last = b[:, :, -1:, :]
        new_state = jnp.exp(b_last[:, :, 0, :])[..., None] * state + \
            jnp.einsum('bhsk,bhsv->bhkv', kc * jnp.exp(b_last - b), vc)
        return new_state, o_intra + o_inter

    s0 = jnp.zeros((B, HG_HEADS, HG_KEY_DIM, HG_VAL_DIM), jnp.float32)
    _, o = lax.scan(step, s0, (to_chunks(q), to_chunks(key), to_chunks(val), to_chunks(log_f)))
    o = o.transpose(1, 0, 3, 2, 4).reshape(B, S, HG_HEADS, HG_VAL_DIM)
    o = rms_norm(o, norm_g)
    gate = jax.nn.silu(g_in.astype(jnp.float32)).reshape(B, S, HG_HEADS, HG_VAL_DIM)
    return (o * gate).reshape(B, S, HG_WIDTH).astype(q_in.dtype)


def hybrid_layer(x, norm1_g, w_in, q_norm_g, k_norm_g, lb, hg_norm_g, w_branch, w_out,
                 norm2_g, w_gate, w_up, w_down):
    B, S, _ = x.shape
    xn = rms_norm(x, norm1_g)
    proj = xn @ w_in
    q_sb, k_sb, v_sb, q_hg, f_hg, i_hg, g_hg, gate_sb, gate_hg = split_cols(proj, IN_SIZES)

    def heads(t):
        return t.reshape(B, S, SB_HEADS, SB_HEAD_DIM).transpose(0, 2, 1, 3)

    q = rms_norm(heads(q_sb), q_norm_g)
    k = rms_norm(heads(k_sb), k_norm_g)
    o_sb = stick_breaking_attention(q, k, heads(v_sb))
    o_sb = o_sb.transpose(0, 2, 1, 3).reshape(B, S, SB_WIDTH)

    o_hg = hgrn2_mixer(q_hg, f_hg, i_hg, g_hg, lb, hg_norm_g)

    y_sb = o_sb @ w_branch[:SB_WIDTH]
    y_hg = o_hg @ w_branch[SB_WIDTH:]
    mixed = jax.nn.sigmoid(gate_sb) * y_sb + jax.nn.sigmoid(gate_hg) * y_hg
    h = x + mixed @ w_out

    hn = rms_norm(h, norm2_g)
    ffn = (jax.nn.silu(hn @ w_gate) * (hn @ w_up)) @ w_down
    return h + ffn


def setup_inputs(seed: int = 0) -> dict:
    key = jax.random.key(seed)
    ks = jax.random.split(key, 13)
    n = jax.random.normal
    f32 = jnp.float32
    return {
        "x": n(ks[0], (BATCH, SEQ, D_MODEL), f32),
        "norm1_g": 1.0 + 0.02 * n(ks[1], (DEPTH, D_MODEL), f32),
        "w_in": n(ks[2], (DEPTH, D_MODEL, IN_WIDTH), f32) * D_MODEL ** -0.5,
        "q_norm_g": 1.0 + 0.02 * n(ks[3], (DEPTH, SB_HEAD_DIM), f32),
        "k_norm_g": 1.0 + 0.02 * n(ks[4], (DEPTH, SB_HEAD_DIM), f32),
        "lb_table": 0.5 * n(ks[5], (DEPTH + 1, HG_KEY_WIDTH), f32),
        "hg_norm_g": 1.0 + 0.02 * n(ks[6], (DEPTH, HG_VAL_DIM), f32),
        "w_branch": n(ks[7], (DEPTH, MIX_WIDTH, D_MODEL), f32) * SB_WIDTH ** -0.5,
        "w_out": n(ks[8], (DEPTH, D_MODEL, D_MODEL), f32) * D_MODEL ** -0.5,
        "norm2_g": 1.0 + 0.02 * n(ks[9], (DEPTH, D_MODEL), f32),
        "w_ffn_gate": n(ks[10], (DEPTH, D_MODEL, FFN_HIDDEN), f32) * D_MODEL ** -0.5,
        "w_ffn_up": n(ks[11], (DEPTH, D_MODEL, FFN_HIDDEN), f32) * D_MODEL ** -0.5,
        "w_ffn_down": n(ks[12], (DEPTH, FFN_HIDDEN, D_MODEL), f32) * FFN_HIDDEN ** -0.5,
    }


def reference(x, norm1_g, w_in, q_norm_g, k_norm_g, lb_table, hg_norm_g, w_branch, w_out,
              norm2_g, w_ffn_gate, w_ffn_up, w_ffn_down):
    lower_bounds = jnp.cumsum(jax.nn.softmax(lb_table.astype(jnp.float32), axis=0), axis=0)
    h = x
    for layer in range(DEPTH):
        h = hybrid_layer(h, norm1_g[layer], w_in[layer], q_norm_g[layer], k_norm_g[layer],
                         lower_bounds[layer], hg_norm_g[layer], w_branch[layer], w_out[layer],
                         norm2_g[layer], w_ffn_gate[layer], w_ffn_up[layer], w_ffn_down[layer])
    return h
```

```python
import functools

import numpy as np
import jax
import jax.numpy as jnp
from jax import lax
from jax.experimental import pallas as pl
from jax.experimental.pallas import tpu as pltpu

F32 = jnp.float32
BF16 = jnp.bfloat16
EPS = 1e-6

SB_HEAD_DIM = 64
HG_KEY_DIM = 128
HG_VAL_DIM = 128
LANES = 128

VMEM_LIMIT = 56 * 1024 * 1024


def _sigmoid(t):
    return 1.0 / (1.0 + jnp.exp(-t))


def _dot(a, b):
    return jnp.dot(a, b, preferred_element_type=F32)


def _dot_nt(a, b):
    return lax.dot_general(a, b, (((1,), (1,)), ((), ())), preferred_element_type=F32)


def _dot_tn(a, b):
    return lax.dot_general(a, b, (((0,), (0,)), ((), ())), preferred_element_type=F32)


def _in_proj_kernel(x_ref, g1_ref, w_ref, qg_ref, kg_ref, lbt_ref, hm_ref,
                    q_ref, k_ref, v_ref, hq_ref, hk_ref, hlf_ref, hv_ref, hg_ref,
                    gsb_ref, ghg_ref, *, sbw, hgk, hgw, d):
    x = x_ref[...]
    ms = jnp.mean(x * x, axis=-1, keepdims=True)
    xn = (x * lax.rsqrt(ms + EPS) * g1_ref[...]).astype(BF16)

    col = [0]

    def proj(width):
        c0 = col[0]
        col[0] = c0 + width
        return _dot(xn, w_ref[:, c0:c0 + width])

    def head_norm(t, g):
        hms = _dot((t * t).astype(BF16), hm_ref[...])
        return t * lax.rsqrt(hms + EPS) * g

    q_ref[...] = head_norm(proj(sbw), qg_ref[...]).astype(BF16)
    k_ref[...] = head_norm(proj(sbw), kg_ref[...]).astype(BF16)
    v_ref[...] = proj(sbw).astype(BF16)

    hq = proj(hgk)
    hq_ref[...] = (hq * _sigmoid(hq)).astype(BF16)

    lbt = lbt_ref[...]
    e = jnp.exp(lbt - jnp.max(lbt, axis=0, keepdims=True))
    lb = e[0:1, :] / jnp.sum(e, axis=0, keepdims=True)
    f = lb + (1.0 - lb) * _sigmoid(proj(hgk))
    hlf_ref[...] = jnp.log(f)
    hk_ref[...] = (1.0 - f).astype(BF16)

    hv_ref[...] = proj(hgw).astype(BF16)
    hg = proj(hgw)
    hg_ref[...] = (hg * _sigmoid(hg)).astype(BF16)
    gsb_ref[...] = _sigmoid(proj(d)).astype(BF16)
    ghg_ref[...] = _sigmoid(proj(d)).astype(BF16)


def _in_proj(x2, g1, w_in, qg, kg, lbt, *, sbw, hgk, hgw, tm):
    n, d = x2.shape
    in_w = w_in.shape[1]
    assert in_w == 3 * sbw + 2 * hgk + 2 * hgw + 2 * d
    heads = np.arange(sbw) // SB_HEAD_DIM
    hm = jnp.asarray((heads[:, None] == heads[None, :]) / SB_HEAD_DIM, BF16)

    def rows(width):
        return pl.BlockSpec((tm, width), lambda i: (i, 0))

    def const(shape):
        return pl.BlockSpec(shape, lambda i: (0, 0), pipeline_mode=pl.Buffered(1))

    out_shapes = (
        jax.ShapeDtypeStruct((n, sbw), BF16), jax.ShapeDtypeStruct((n, sbw), BF16),
        jax.ShapeDtypeStruct((n, sbw), BF16),
        jax.ShapeDtypeStruct((n, hgk), BF16), jax.ShapeDtypeStruct((n, hgk), BF16),
        jax.ShapeDtypeStruct((n, hgk), F32),
        jax.ShapeDtypeStruct((n, hgw), BF16), jax.ShapeDtypeStruct((n, hgw), BF16),
        jax.ShapeDtypeStruct((n, d), BF16), jax.ShapeDtypeStruct((n, d), BF16),
    )
    out_specs = (rows(sbw), rows(sbw), rows(sbw), rows(hgk), rows(hgk), rows(hgk),
                 rows(hgw), rows(hgw), rows(d), rows(d))
    return pl.pallas_call(
        functools.partial(_in_proj_kernel, sbw=sbw, hgk=hgk, hgw=hgw, d=d),
        grid=(n // tm,),
        in_specs=[rows(d), const((1, d)), const((d, in_w)), const((1, sbw)), const((1, sbw)),
                  const(lbt.shape), const((sbw, sbw))],
        out_specs=out_specs,
        out_shape=out_shapes,
        compiler_params=pltpu.CompilerParams(
            dimension_semantics=("arbitrary",), vmem_limit_bytes=VMEM_LIMIT),
        name="in_proj",
    )(x2, g1, w_in, qg, kg, lbt, hm)


def _sb_attn_kernel(q_ref, k_ref, v_ref, t_ref, o_ref, *, tq, tk):
    i = pl.program_id(2)
    q = q_ref[0]
    lane = lax.broadcasted_iota(jnp.int32, (tk, LANES), 1)
    head0 = lane < SB_HEAD_DIM
    zero = jnp.zeros((tk, LANES), BF16)
    row = lax.broadcasted_iota(jnp.int32, (tq, 2 * tk), 0)
    colk = lax.broadcasted_iota(jnp.int32, (tq, 2 * tk), 1) % tk
    per_q = tq // tk

    def step(j, carry, masked_offset):
        acc, out = carry
        ks = pl.multiple_of(j * tk, tk)
        kb = k_ref[0, pl.ds(ks, tk), :]
        vb = v_ref[0, pl.ds(ks, tk), :]
        rhs_k = jnp.concatenate([jnp.where(head0, kb, zero), jnp.where(head0, zero, kb)], axis=0)
        rhs_v = jnp.concatenate([jnp.where(head0, vb, zero), jnp.where(head0, zero, vb)], axis=0)
        z = _dot_nt(q, rhs_k)
        sp = jnp.maximum(z, 0.0) + jnp.log(1.0 + jnp.exp(-jnp.abs(z)))
        if masked_offset is not None:
            causal = colk + masked_offset < row
            sp = jnp.where(causal, sp, 0.0)
        rest = _dot(sp.astype(BF16), t_ref[...])
        u = z - sp - rest - acc
        if masked_offset is not None:
            u = jnp.where(causal, u, -1e30)
        w = jnp.exp(u)
        out = out + _dot(w.astype(BF16), rhs_v)
        tot0 = rest[:, 0:1] + sp[:, 0:1]
        tot1 = rest[:, tk:tk + 1] + sp[:, tk:tk + 1]
        acc = acc + jnp.concatenate([jnp.broadcast_to(tot0, (tq, tk)),
                                     jnp.broadcast_to(tot1, (tq, tk))], axis=1)
        return acc, out

    carry = (jnp.zeros((tq, 2 * tk), F32), jnp.zeros((tq, LANES), F32))
    for dd in reversed(range(per_q)):
        carry = step(i * per_q + dd, carry, dd * tk)

    n_full = i * per_q

    def body(it, c):
        return step(n_full - 1 - it, c, None)

    _, out = lax.fori_loop(0, n_full, body, carry)
    o_ref[0] = out.astype(BF16)


def _sb_attention(q, k, v, *, tq, tk):
    b, s, w = q.shape
    pairs = w // LANES
    jj = np.arange(2 * tk)
    tmat = jnp.asarray((jj[:, None] // tk == jj[None, :] // tk) & (jj[:, None] > jj[None, :]), BF16)
    return pl.pallas_call(
        functools.partial(_sb_attn_kernel, tq=tq, tk=tk),
        grid=(b, pairs, s // tq),
        in_specs=[pl.BlockSpec((1, tq, LANES), lambda bi, p, i: (bi, i, p)),
                  pl.BlockSpec((1, s, LANES), lambda bi, p, i: (bi, 0, p)),
                  pl.BlockSpec((1, s, LANES), lambda bi, p, i: (bi, 0, p)),
                  pl.BlockSpec((2 * tk, 2 * tk), lambda bi, p, i: (0, 0))],
        out_specs=pl.BlockSpec((1, tq, LANES), lambda bi, p, i: (bi, i, p)),
        out_shape=jax.ShapeDtypeStruct((b, s, w), BF16),
        compiler_params=pltpu.CompilerParams(
            dimension_semantics=("arbitrary", "arbitrary", "arbitrary"),
            vmem_limit_bytes=VMEM_LIMIT),
        name="sb_attention",
    )(q, k, v, tmat)


def _hgrn_tables(c):
    levels = int(np.log2(c))
    assert 2 ** levels == c
    t = np.arange(c)[:, None]
    j = np.arange(c)[None, :]
    blocks = []
    masks = [np.eye(c)]
    for l in range(1, levels + 1):
        m = 2 ** l
        r = (t // m) * m + m // 2
        upper = (t % m) >= m // 2
        blocks.append(upper & (j >= r) & (j <= t))
        blocks.append((~upper) & (j > t) & (j < r))
        ts = np.arange(c)[:, None]
        ss = np.arange(c)[None, :]
        masks.append((ts // m == ss // m) & ((ts % m) >= m // 2) & ((ss % m) < m // 2))
    blocks.append(j <= t)
    blocks.append(j > t)
    mcat = np.concatenate(blocks, axis=0).astype(np.float32)
    mcat = np.concatenate([mcat, mcat], axis=1)
    return jnp.asarray(mcat, BF16), jnp.asarray(np.stack(masks).astype(np.float32), F32), levels


def _hgrn_kernel(q_ref, k_ref, lf_ref, v_ref, g_ref, mcat_ref, mask_ref, ng_ref, o_ref,
                 state_ref, *, c, levels, heads):
    @pl.when(pl.program_id(1) == 0)
    def _():
        state_ref[...] = jnp.zeros_like(state_ref)

    lf = lf_ref[0]
    hi = lf.astype(BF16)
    lo = (lf - hi.astype(F32)).astype(BF16)
    hilo = jnp.concatenate([hi, lo], axis=0)
    dk, dv = HG_KEY_DIM, HG_VAL_DIM

    for h in range(heads):
        ksl = slice(h * dk, (h + 1) * dk)
        vsl = slice(h * dv, (h + 1) * dv)
        ex = jnp.exp(_dot(mcat_ref[...], hilo[:, ksl]))
        qh = q_ref[0, :, ksl].astype(F32)
        kh = k_ref[0, :, ksl].astype(F32)
        vh = v_ref[0, :, vsl]

        attn = mask_ref[0] * _dot_nt(qh.astype(BF16), kh.astype(BF16))
        for l in range(1, levels + 1):
            ql = (qh * ex[(2 * l - 2) * c:(2 * l - 1) * c]).astype(BF16)
            kl = (kh * ex[(2 * l - 1) * c:(2 * l) * c]).astype(BF16)
            attn = attn + mask_ref[l] * _dot_nt(ql, kl)

        eb = ex[2 * levels * c:(2 * levels + 1) * c]
        qb = (qh * eb).astype(BF16)
        kb = (kh * ex[(2 * levels + 1) * c:(2 * levels + 2) * c]).astype(BF16)
        st = state_ref[h]
        o = _dot(attn.astype(BF16), vh) + _dot_nt(qb, st.astype(BF16))
        state_ref[h] = st * eb[c - 1:c, :] + _dot_tn(vh, kb)

        ms = jnp.mean(o * o, axis=-1, keepdims=True)
        o = o * lax.rsqrt(ms + EPS) * ng_ref[...]
        o_ref[0, :, vsl] = (o * g_ref[0, :, vsl].astype(F32)).astype(BF16)


def _hgrn(hq, hk, hlf, hv, hg, ng, *, c):
    b, s, kw = hq.shape
    vw = hv.shape[2]
    heads = kw // HG_KEY_DIM
    mcat, masks, levels = _hgrn_tables(c)

    def seq(width):
        return pl.BlockSpec((1, c, width), lambda bi, ci: (bi, ci, 0))

    return pl.pallas_call(
        functools.partial(_hgrn_kernel, c=c, levels=levels, heads=heads),
        grid=(b, s // c),
        in_specs=[seq(kw), seq(kw), seq(kw), seq(vw), seq(vw),
                  pl.BlockSpec(mcat.shape, lambda bi, ci: (0, 0)),
                  pl.BlockSpec(masks.shape, lambda bi, ci: (0, 0, 0)),
                  pl.BlockSpec((1, HG_VAL_DIM), lambda bi, ci: (0, 0))],
        out_specs=seq(vw),
        out_shape=jax.ShapeDtypeStruct((b, s, vw), BF16),
        scratch_shapes=[pltpu.VMEM((heads, HG_VAL_DIM, HG_KEY_DIM), F32)],
        compiler_params=pltpu.CompilerParams(
            dimension_semantics=("arbitrary", "arbitrary"), vmem_limit_bytes=VMEM_LIMIT),
        name="hgrn2",
    )(hq, hk, hlf, hv, hg, mcat, masks, ng)


def _tail_kernel(x_ref, osb_ref, ohg_ref, gsb_ref, ghg_ref, wb_ref, wo_ref, g2_ref,
                 wg_ref, wu_ref, wd_ref, out_ref, *, sbw, ffn_chunks):
    ysb = _dot(osb_ref[...], wb_ref[:sbw, :])
    yhg = _dot(ohg_ref[...], wb_ref[sbw:, :])
    mixed = gsb_ref[...].astype(F32) * ysb + ghg_ref[...].astype(F32) * yhg
    h = x_ref[...] + _dot(mixed.astype(BF16), wo_ref[...])
    ms = jnp.mean(h * h, axis=-1, keepdims=True)
    hn = (h * lax.rsqrt(ms + EPS) * g2_ref[...]).astype(BF16)
    acc = h
    for c0, c1 in ffn_chunks:
        a = _dot(hn, wg_ref[:, c0:c1])
        u = _dot(hn, wu_ref[:, c0:c1])
        acc = acc + _dot((a * _sigmoid(a) * u).astype(BF16), wd_ref[c0:c1, :])
    out_ref[...] = acc


def _tail(x2, osb, ohg, gsb, ghg, wb, wo, g2, wg, wu, wd, *, tm):
    n, d = x2.shape
    sbw = osb.shape[1]
    hgw = ohg.shape[1]
    f = wg.shape[1]
    half = (f // 2) // 256 * 256
    ffn_chunks = ((0, half), (half, f)) if 0 < half < f else ((0, f),)

    def rows(width):
        return pl.BlockSpec((tm, width), lambda i: (i, 0))

    def const(shape):
        return pl.BlockSpec(shape, lambda i: (0, 0), pipeline_mode=pl.Buffered(1))

    return pl.pallas_call(
        functools.partial(_tail_kernel, sbw=sbw, ffn_chunks=ffn_chunks),
        grid=(n // tm,),
        in_specs=[rows(d), rows(sbw), rows(hgw), rows(d), rows(d),
                  const(wb.shape), const(wo.shape), const((1, d)),
                  const(wg.shape), const(wu.shape), const(wd.shape)],
        out_specs=rows(d),
        out_shape=jax.ShapeDtypeStruct((n, d), F32),
        compiler_params=pltpu.CompilerParams(
            dimension_semantics=("arbitrary",), vmem_limit_bytes=VMEM_LIMIT),
        name="tail",
    )(x2, osb, ohg, gsb, ghg, wb, wo, g2, wg, wu, wd)


def _layer(x, norm1_g, w_in, q_norm_g, k_norm_g, lb_table, hg_norm_g, w_branch, w_out,
           norm2_g, w_gate, w_up, w_down, *, tm_in, tq, tk, chunk, tm_tail):
    b, s, d = x.shape
    hgk = lb_table.shape[1]
    mix = w_branch.shape[0]
    in_w = w_in.shape[1]
    sbw = in_w - 2 * hgk - 2 * d - 2 * mix
    hgw = mix - sbw
    assert sbw % LANES == 0 and hgk % HG_KEY_DIM == 0 and hgw % HG_VAL_DIM == 0
    sb_heads = sbw // SB_HEAD_DIM

    x2 = x.reshape(b * s, d)
    scale = SB_HEAD_DIM ** -0.5
    qg = jnp.tile(q_norm_g.astype(F32) * scale, sb_heads).reshape(1, sbw)
    kg = jnp.tile(k_norm_g.astype(F32), sb_heads).reshape(1, sbw)

    q, k, v, hq, hk, hlf, hv, hg, gsb, ghg = _in_proj(
        x2, norm1_g.reshape(1, d), w_in.astype(BF16), qg, kg, lb_table.astype(F32),
        sbw=sbw, hgk=hgk, hgw=hgw, tm=tm_in)

    def seq(t):
        return t.reshape(b, s, t.shape[-1])

    o_sb = _sb_attention(seq(q), seq(k), seq(v), tq=tq, tk=tk)
    o_hg = _hgrn(seq(hq), seq(hk), seq(hlf), seq(hv), seq(hg),
                 hg_norm_g.reshape(1, HG_VAL_DIM).astype(F32), c=chunk)

    out = _tail(x2, o_sb.reshape(b * s, sbw), o_hg.reshape(b * s, hgw), gsb, ghg,
                w_branch.astype(BF16), w_out.astype(BF16), norm2_g.reshape(1, d),
                w_gate.astype(BF16), w_up.astype(BF16), w_down.astype(BF16), tm=tm_tail)
    return out.reshape(b, s, d)


def kernel(x, norm1_g, w_in, q_norm_g, k_norm_g, lb_table, hg_norm_g, w_branch, w_out,
           norm2_g, w_ffn_gate, w_ffn_up, w_ffn_down):
    depth = w_in.shape[0]
    assert depth == 1 and lb_table.shape[0] == depth + 1
    return _layer(x, norm1_g[0], w_in[0], q_norm_g[0], k_norm_g[0], lb_table, hg_norm_g[0],
                  w_branch[0], w_out[0], norm2_g[0], w_ffn_gate[0], w_ffn_up[0], w_ffn_down[0],
                  tm_in=512, tq=256, tk=128, chunk=128, tm_tail=512)
```

```python
import functools

import numpy as np
import jax
import jax.numpy as jnp
from jax import lax
from jax.experimental import pallas as pl
from jax.experimental.pallas import tpu as pltpu

F32 = jnp.float32
BF16 = jnp.bfloat16
EPS = 1e-6
LOG2E = 1.4426950408889634

SB_HEAD_DIM = 64
HG_KEY_DIM = 128
HG_VAL_DIM = 128
LANES = 128

VMEM_LIMIT = 56 * 1024 * 1024


def _sigmoid(t):
    return 1.0 / (1.0 + jnp.exp(-t))


def _dot(a, b):
    return jnp.dot(a, b, preferred_element_type=F32)


def _dot_nt(a, b):
    return lax.dot_general(a, b, (((1,), (1,)), ((), ())), preferred_element_type=F32)


def _dot_tn(a, b):
    return lax.dot_general(a, b, (((0,), (0,)), ((), ())), preferred_element_type=F32)


def _in_proj_kernel(x_ref, g1_ref, w_ref, qg_ref, kg_ref, lbt_ref, hm_ref,
                    q_ref, k_ref, v_ref, hq_ref, hk_ref, hlf_ref, hv_ref, hg_ref,
                    gsb_ref, ghg_ref, *, sbw, hgk, hgw, d):
    x = x_ref[...]
    ms = jnp.mean(x * x, axis=-1, keepdims=True)
    xn = (x * lax.rsqrt(ms + EPS) * g1_ref[...]).astype(BF16)

    col = [0]

    def proj(width):
        c0 = col[0]
        col[0] = c0 + width
        return _dot(xn, w_ref[:, c0:c0 + width])

    def head_norm(t, g):
        hms = _dot((t * t).astype(BF16), hm_ref[...])
        return t * lax.rsqrt(hms + EPS) * g

    q_ref[...] = head_norm(proj(sbw), qg_ref[...]).astype(BF16)
    k_ref[...] = head_norm(proj(sbw), kg_ref[...]).astype(BF16)
    v_ref[...] = proj(sbw).astype(BF16)

    hq = proj(hgk)
    hq_ref[...] = (hq * _sigmoid(hq)).astype(BF16)

    lbt = lbt_ref[...]
    e = jnp.exp(lbt - jnp.max(lbt, axis=0, keepdims=True))
    lb = e[0:1, :] / jnp.sum(e, axis=0, keepdims=True)
    f = lb + (1.0 - lb) * _sigmoid(proj(hgk))
    hlf_ref[...] = jnp.log(f)
    hk_ref[...] = (1.0 - f).astype(BF16)

    hv_ref[...] = proj(hgw).astype(BF16)
    hg = proj(hgw)
    hg_ref[...] = (hg * _sigmoid(hg)).astype(BF16)
    gsb_ref[...] = _sigmoid(proj(d)).astype(BF16)
    ghg_ref[...] = _sigmoid(proj(d)).astype(BF16)


def _in_proj(x2, g1, w_in, qg, kg, lbt, *, sbw, hgk, hgw, tm):
    n, d = x2.shape
    in_w = w_in.shape[1]
    assert in_w == 3 * sbw + 2 * hgk + 2 * hgw + 2 * d
    heads = np.arange(sbw) // SB_HEAD_DIM
    hm = jnp.asarray((heads[:, None] == heads[None, :]) / SB_HEAD_DIM, BF16)

    def rows(width):
        return pl.BlockSpec((tm, width), lambda i: (i, 0))

    def const(shape):
        return pl.BlockSpec(shape, lambda i: (0, 0), pipeline_mode=pl.Buffered(1))

    out_shapes = (
        jax.ShapeDtypeStruct((n, sbw), BF16), jax.ShapeDtypeStruct((n, sbw), BF16),
        jax.ShapeDtypeStruct((n, sbw), BF16),
        jax.ShapeDtypeStruct((n, hgk), BF16), jax.ShapeDtypeStruct((n, hgk), BF16),
        jax.ShapeDtypeStruct((n, hgk), F32),
        jax.ShapeDtypeStruct((n, hgw), BF16), jax.ShapeDtypeStruct((n, hgw), BF16),
        jax.ShapeDtypeStruct((n, d), BF16), jax.ShapeDtypeStruct((n, d), BF16),
    )
    out_specs = (rows(sbw), rows(sbw), rows(sbw), rows(hgk), rows(hgk), rows(hgk),
                 rows(hgw), rows(hgw), rows(d), rows(d))
    return pl.pallas_call(
        functools.partial(_in_proj_kernel, sbw=sbw, hgk=hgk, hgw=hgw, d=d),
        grid=(n // tm,),
        in_specs=[rows(d), const((1, d)), const((d, in_w)), const((1, sbw)), const((1, sbw)),
                  const(lbt.shape), const((sbw, sbw))],
        out_specs=out_specs,
        out_shape=out_shapes,
        compiler_params=pltpu.CompilerParams(
            dimension_semantics=("arbitrary",), vmem_limit_bytes=VMEM_LIMIT),
        name="in_proj",
    )(x2, g1, w_in, qg, kg, lbt, hm)


def _sb_attn_kernel(q_ref, k_ref, v_ref, t_ref, o_ref, sp_ref, z_ref, u_ref, acc_ref, out_ref,
                    *, tq, tk, group):
    i = pl.program_id(2)
    assert tq == 2 * tk
    n_blocks = 2 * i + 2
    lane = lax.broadcasted_iota(jnp.int32, (tk, LANES), 1)
    head0 = lane < SB_HEAD_DIM
    zero = jnp.zeros((tk, LANES), BF16)
    row = lax.broadcasted_iota(jnp.int32, (tq, 2 * tk), 0)
    colk = lax.broadcasted_iota(jnp.int32, (tq, 2 * tk), 1) % tk
    sign = jnp.uint32(0x80000000)
    pair_lanes = [slice(g * LANES, (g + 1) * LANES) for g in range(group)]

    def key_start(n):
        return pl.multiple_of((n_blocks - 1 - n) * tk, tk)

    def block_diag(ref, n, psl):
        blk = ref[0, pl.ds(key_start(n), tk), psl]
        return jnp.concatenate([jnp.where(head0, blk, zero), jnp.where(head0, zero, blk)], axis=0)

    def logits(n, slot, diag_offset=None):
        for g, psl in enumerate(pair_lanes):
            z = _dot_nt(q_ref[0, :, psl], block_diag(k_ref, n, psl))
            neg_abs = lax.bitcast_convert_type(lax.bitcast_convert_type(z, jnp.uint32) | sign, F32)
            sp = jnp.maximum(z, 0.0) + jnp.log(1.0 + jnp.exp2(neg_abs)) * LOG2E
            if diag_offset is not None:
                sp = jnp.where(colk + diag_offset < row, sp, 0.0)
            sp_ref[slot, g] = sp.astype(BF16)
            z_ref[slot, g] = z

    def exponent(n, slot, diag_offset=None):
        for g in range(group):
            total = _dot(sp_ref[slot, g], t_ref[...]) + acc_ref[g]
            u = z_ref[slot, g] - total
            if diag_offset is not None:
                u = jnp.where(colk + diag_offset < row, u, -1e30)
            u_ref[slot, g] = u
            acc_ref[g] = jnp.concatenate(
                [jnp.broadcast_to(total[:, 0:1], (tq, tk)),
                 jnp.broadcast_to(total[:, tk:tk + 1], (tq, tk))], axis=1)

    def weights(n, slot):
        for g, psl in enumerate(pair_lanes):
            w = jnp.exp2(u_ref[slot, g]).astype(BF16)
            out_ref[g] += _dot(w, block_diag(v_ref, n, psl))

    acc_ref[...] = jnp.zeros_like(acc_ref)
    out_ref[...] = jnp.zeros_like(out_ref)

    logits(0, 0, tk)
    exponent(0, 0, tk)
    logits(1, 1, 0)

    @pl.when(i == 0)
    def _():
        weights(0, 0)
        exponent(1, 1, 0)
        weights(1, 1)

    @pl.when(i > 0)
    def _():
        weights(0, 0)
        exponent(1, 1, 0)
        logits(2, 0)
        weights(1, 1)
        exponent(2, 0)
        logits(3, 1)

        def body(it, carry):
            n = 4 + 2 * it
            weights(n - 2, 0)
            exponent(n - 1, 1)
            logits(n, 0)
            weights(n - 1, 1)
            exponent(n, 0)
            logits(n + 1, 1)
            return carry

        lax.fori_loop(0, i - 1, body, 0)
        weights(n_blocks - 2, 0)
        exponent(n_blocks - 1, 1)
        weights(n_blocks - 1, 1)

    for g, psl in enumerate(pair_lanes):
        o_ref[0, :, psl] = out_ref[g].astype(BF16)


def _sb_attention(q, k, v, *, tq, tk, group):
    b, s, w = q.shape
    gw = group * LANES
    assert w % gw == 0
    jj = np.arange(2 * tk)
    tmat = jnp.asarray((jj[:, None] // tk == jj[None, :] // tk) & (jj[:, None] >= jj[None, :]), BF16)
    return pl.pallas_call(
        functools.partial(_sb_attn_kernel, tq=tq, tk=tk, group=group),
        grid=(b, w // gw, s // tq),
        in_specs=[pl.BlockSpec((1, tq, gw), lambda bi, p, i: (bi, i, p)),
                  pl.BlockSpec((1, s, gw), lambda bi, p, i: (bi, 0, p)),
                  pl.BlockSpec((1, s, gw), lambda bi, p, i: (bi, 0, p)),
                  pl.BlockSpec((2 * tk, 2 * tk), lambda bi, p, i: (0, 0))],
        out_specs=pl.BlockSpec((1, tq, gw), lambda bi, p, i: (bi, i, p)),
        out_shape=jax.ShapeDtypeStruct((b, s, w), BF16),
        scratch_shapes=[pltpu.VMEM((2, group, tq, 2 * tk), BF16),
                        pltpu.VMEM((2, group, tq, 2 * tk), F32),
                        pltpu.VMEM((2, group, tq, 2 * tk), F32),
                        pltpu.VMEM((group, tq, 2 * tk), F32),
                        pltpu.VMEM((group, tq, LANES), F32)],
        compiler_params=pltpu.CompilerParams(
            dimension_semantics=("arbitrary", "arbitrary", "arbitrary"),
            vmem_limit_bytes=VMEM_LIMIT),
        name="sb_attention",
    )(q, k, v, tmat)


def _hgrn_tables(c):
    levels = int(np.log2(c))
    assert 2 ** levels == c
    t = np.arange(c)[:, None]
    j = np.arange(c)[None, :]
    blocks = []
    masks = [np.eye(c)]
    for l in range(1, levels + 1):
        m = 2 ** l
        r = (t // m) * m + m // 2
        upper = (t % m) >= m // 2
        blocks.append(upper & (j >= r) & (j <= t))
        blocks.append((~upper) & (j > t) & (j < r))
        ts = np.arange(c)[:, None]
        ss = np.arange(c)[None, :]
        masks.append((ts // m == ss // m) & ((ts % m) >= m // 2) & ((ss % m) < m // 2))
    blocks.append(j <= t)
    blocks.append(j > t)
    mcat = np.concatenate(blocks, axis=0).astype(np.float32)
    mcat = np.concatenate([mcat, mcat], axis=1)
    return jnp.asarray(mcat, BF16), jnp.asarray(np.stack(masks).astype(np.float32), F32), levels


def _hgrn_kernel(q_ref, k_ref, lf_ref, v_ref, g_ref, mcat_ref, mask_ref, ng_ref, o_ref,
                 state_ref, *, c, levels, heads):
    @pl.when(pl.program_id(1) == 0)
    def _():
        state_ref[...] = jnp.zeros_like(state_ref)

    lf = lf_ref[0]
    hi = lf.astype(BF16)
    lo = (lf - hi.astype(F32)).astype(BF16)
    hilo = jnp.concatenate([hi, lo], axis=0)
    dk, dv = HG_KEY_DIM, HG_VAL_DIM

    for h in range(heads):
        ksl = slice(h * dk, (h + 1) * dk)
        vsl = slice(h * dv, (h + 1) * dv)
        ex = jnp.exp(_dot(mcat_ref[...], hilo[:, ksl]))
        qh = q_ref[0, :, ksl].astype(F32)
        kh = k_ref[0, :, ksl].astype(F32)
        vh = v_ref[0, :, vsl]

        attn = mask_ref[0] * _dot_nt(qh.astype(BF16), kh.astype(BF16))
        for l in range(1, levels + 1):
            ql = (qh * ex[(2 * l - 2) * c:(2 * l - 1) * c]).astype(BF16)
            kl = (kh * ex[(2 * l - 1) * c:(2 * l) * c]).astype(BF16)
            attn = attn + mask_ref[l] * _dot_nt(ql, kl)

        eb = ex[2 * levels * c:(2 * levels + 1) * c]
        qb = (qh * eb).astype(BF16)
        kb = (kh * ex[(2 * levels + 1) * c:(2 * levels + 2) * c]).astype(BF16)
        st = state_ref[h]
        o = _dot(attn.astype(BF16), vh) + _dot_nt(qb, st.astype(BF16))
        state_ref[h] = st * eb[c - 1:c, :] + _dot_tn(vh, kb)

        ms = jnp.mean(o * o, axis=-1, keepdims=True)
        o = o * lax.rsqrt(ms + EPS) * ng_ref[...]
        o_ref[0, :, vsl] = (o * g_ref[0, :, vsl].astype(F32)).astype(BF16)


def _hgrn(hq, hk, hlf, hv, hg, ng, *, c):
    b, s, kw = hq.shape
    vw = hv.shape[2]
    heads = kw // HG_KEY_DIM
    mcat, masks, levels = _hgrn_tables(c)

    def seq(width):
        return pl.BlockSpec((1, c, width), lambda bi, ci: (bi, ci, 0))

    return pl.pallas_call(
        functools.partial(_hgrn_kernel, c=c, levels=levels, heads=heads),
        grid=(b, s // c),
        in_specs=[seq(kw), seq(kw), seq(kw), seq(vw), seq(vw),
                  pl.BlockSpec(mcat.shape, lambda bi, ci: (0, 0)),
                  pl.BlockSpec(masks.shape, lambda bi, ci: (0, 0, 0)),
                  pl.BlockSpec((1, HG_VAL_DIM), lambda bi, ci: (0, 0))],
        out_specs=seq(vw),
        out_shape=jax.ShapeDtypeStruct((b, s, vw), BF16),
        scratch_shapes=[pltpu.VMEM((heads, HG_VAL_DIM, HG_KEY_DIM), F32)],
        compiler_params=pltpu.CompilerParams(
            dimension_semantics=("arbitrary", "arbitrary"), vmem_limit_bytes=VMEM_LIMIT),
        name="hgrn2",
    )(hq, hk, hlf, hv, hg, mcat, masks, ng)


def _tail_kernel(x_ref, osb_ref, ohg_ref, gsb_ref, ghg_ref, wb_ref, wo_ref, g2_ref,
                 wg_ref, wu_ref, wd_ref, out_ref, *, sbw, ffn_chunks):
    ysb = _dot(osb_ref[...], wb_ref[:sbw, :])
    yhg = _dot(ohg_ref[...], wb_ref[sbw:, :])
    mixed = gsb_ref[...].astype(F32) * ysb + ghg_ref[...].astype(F32) * yhg
    h = x_ref[...] + _dot(mixed.astype(BF16), wo_ref[...])
    ms = jnp.mean(h * h, axis=-1, keepdims=True)
    hn = (h * lax.rsqrt(ms + EPS) * g2_ref[...]).astype(BF16)
    acc = h
    for c0, c1 in ffn_chunks:
        a = _dot(hn, wg_ref[:, c0:c1])
        u = _dot(hn, wu_ref[:, c0:c1])
        acc = acc + _dot((a * _sigmoid(a) * u).astype(BF16), wd_ref[c0:c1, :])
    out_ref[...] = acc


def _tail(x2, osb, ohg, gsb, ghg, wb, wo, g2, wg, wu, wd, *, tm):
    n, d = x2.shape
    sbw = osb.shape[1]
    hgw = ohg.shape[1]
    f = wg.shape[1]
    half = (f // 2) // 256 * 256
    ffn_chunks = ((0, half), (half, f)) if 0 < half < f else ((0, f),)

    def rows(width):
        return pl.BlockSpec((tm, width), lambda i: (i, 0))

    def const(shape):
        return pl.BlockSpec(shape, lambda i: (0, 0), pipeline_mode=pl.Buffered(1))

    return pl.pallas_call(
        functools.partial(_tail_kernel, sbw=sbw, ffn_chunks=ffn_chunks),
        grid=(n // tm,),
        in_specs=[rows(d), rows(sbw), rows(hgw), rows(d), rows(d),
                  const(wb.shape), const(wo.shape), const((1, d)),
                  const(wg.shape), const(wu.shape), const(wd.shape)],
        out_specs=rows(d),
        out_shape=jax.ShapeDtypeStruct((n, d), F32),
        compiler_params=pltpu.CompilerParams(
            dimension_semantics=("arbitrary",), vmem_limit_bytes=VMEM_LIMIT),
        name="tail",
    )(x2, osb, ohg, gsb, ghg, wb, wo, g2, wg, wu, wd)


def _layer(x, norm1_g, w_in, q_norm_g, k_norm_g, lb_table, hg_norm_g, w_branch, w_out,
           norm2_g, w_gate, w_up, w_down, *, tm_in, tq, tk, group, chunk, tm_tail):
    b, s, d = x.shape
    hgk = lb_table.shape[1]
    mix = w_branch.shape[0]
    in_w = w_in.shape[1]
    sbw = in_w - 2 * hgk - 2 * d - 2 * mix
    hgw = mix - sbw
    assert sbw % LANES == 0 and hgk % HG_KEY_DIM == 0 and hgw % HG_VAL_DIM == 0
    sb_heads = sbw // SB_HEAD_DIM

    x2 = x.reshape(b * s, d)
    scale = SB_HEAD_DIM ** -0.5 * LOG2E
    qg =jnp.tile(q_norm_g.astype(F32) * scale, sb_heads).reshape(1, sbw)
    kg = jnp.tile(k_norm_g.astype(F32), sb_heads).reshape(1, sbw)

    q, k, v, hq, hk, hlf, hv, hg, gsb, ghg = _in_proj(
        x2, norm1_g.reshape(1, d), w_in.astype(BF16), qg, kg, lb_table.astype(F32),
        sbw=sbw, hgk=hgk, hgw=hgw, tm=tm_in)

    def seq(t):
        return t.reshape(b, s, t.shape[-1])

    o_sb = _sb_attention(seq(q), seq(k), seq(v), tq=tq, tk=tk, group=group)
    o_hg = _hgrn(seq(hq), seq(hk), seq(hlf), seq(hv), seq(hg),
                 hg_norm_g.reshape(1, HG_VAL_DIM).astype(F32), c=chunk)

    out = _tail(x2, o_sb.reshape(b * s, sbw), o_hg.reshape(b * s, hgw), gsb, ghg,
                w_branch.astype(BF16), w_out.astype(BF16), norm2_g.reshape(1, d),
                w_gate.astype(BF16), w_up.astype(BF16), w_down.astype(BF16), tm=tm_tail)
    return out.reshape(b, s, d)


def kernel(x, norm1_g, w_in, q_norm_g, k_norm_g, lb_table, hg_norm_g, w_branch, w_out,
           norm2_g, w_ffn_gate, w_ffn_up, w_ffn_down):
    depth = w_in.shape[0]
    assert depth == 1 and lb_table.shape[0] == depth + 1
    return _layer(x, norm1_g[0], w_in[0], q_norm_g[0], k_norm_g[0], lb_table, hg_norm_g[0],
                  w_branch[0], w_out[0], norm2_g[0], w_ffn_gate[0], w_ffn_up[0], w_ffn_down[0],
                  tm_in=512, tq=256, tk=128, group=4, chunk=128, tm_tail=512)
```

```python
import functools

import numpy as np
import jax
import jax.numpy as jnp
from jax import lax
from jax.experimental import pallas as pl
from jax.experimental.pallas import tpu as pltpu

F32 = jnp.float32
BF16 = jnp.bfloat16
EPS = 1e-6
LOG2E = 1.4426950408889634

SB_HEAD_DIM = 64
HG_KEY_DIM = 128
HG_VAL_DIM = 128
LANES = 128

VMEM_LIMIT = 56 * 1024 * 1024


def _sigmoid(t):
    return 1.0 / (1.0 + jnp.exp(-t))


def _dot(a, b):
    return jnp.dot(a, b, preferred_element_type=F32)


def _dot_nt(a, b):
    return lax.dot_general(a, b, (((1,), (1,)), ((), ())), preferred_element_type=F32)


def _dot_tn(a, b):
    return lax.dot_general(a, b, (((0,), (0,)), ((), ())), preferred_element_type=F32)


def _in_proj_kernel(x_ref, g1_ref, w_ref, qg_ref, kg_ref, lbt_ref, hm_ref,
                    q_ref, k_ref, v_ref, hq_ref, hk_ref, hlf_ref, hv_ref, hg_ref,
                    gsb_ref, ghg_ref, *, sbw, hgk, hgw, d):
    x = x_ref[...]
    ms = jnp.mean(x * x, axis=-1, keepdims=True)
    xn = (x * lax.rsqrt(ms + EPS) * g1_ref[...]).astype(BF16)

    col = [0]

    def proj(width):
        c0 = col[0]
        col[0] = c0 + width
        return _dot(xn, w_ref[:, c0:c0 + width])

    def head_norm(t, g):
        hms = _dot((t * t).astype(BF16), hm_ref[...])
        return t * lax.rsqrt(hms + EPS) * g

    q_ref[...] = head_norm(proj(sbw), qg_ref[...]).astype(BF16)
    k_ref[...] = head_norm(proj(sbw), kg_ref[...]).astype(BF16)
    v_ref[...] = proj(sbw).astype(BF16)

    hq = proj(hgk)
    hq_ref[...] = (hq * _sigmoid(hq)).astype(BF16)

    lbt = lbt_ref[...]
    e = jnp.exp(lbt - jnp.max(lbt, axis=0, keepdims=True))
    lb = e[0:1, :] / jnp.sum(e, axis=0, keepdims=True)
    f = lb + (1.0 - lb) * _sigmoid(proj(hgk))
    hlf_ref[...] = jnp.log(f)
    hk_ref[...] = (1.0 - f).astype(BF16)

    hv_ref[...] = proj(hgw).astype(BF16)
    hg = proj(hgw)
    hg_ref[...] = (hg * _sigmoid(hg)).astype(BF16)
    gsb_ref[...] = _sigmoid(proj(d)).astype(BF16)
    ghg_ref[...] = _sigmoid(proj(d)).astype(BF16)


def _in_proj(x2, g1, w_in, qg, kg, lbt, *, sbw, hgk, hgw, tm):
    n, d = x2.shape
    in_w = w_in.shape[1]
    assert in_w == 3 * sbw + 2 * hgk + 2 * hgw + 2 * d
    heads = np.arange(sbw) // SB_HEAD_DIM
    hm = jnp.asarray((heads[:, None] == heads[None, :]) / SB_HEAD_DIM, BF16)

    def rows(width):
        return pl.BlockSpec((tm, width), lambda i: (i, 0))

    def const(shape):
        return pl.BlockSpec(shape, lambda i: (0, 0), pipeline_mode=pl.Buffered(1))

    out_shapes = (
        jax.ShapeDtypeStruct((n, sbw), BF16), jax.ShapeDtypeStruct((n, sbw), BF16),
        jax.ShapeDtypeStruct((n, sbw), BF16),
        jax.ShapeDtypeStruct((n, hgk), BF16), jax.ShapeDtypeStruct((n, hgk), BF16),
        jax.ShapeDtypeStruct((n, hgk), F32),
        jax.ShapeDtypeStruct((n, hgw), BF16), jax.ShapeDtypeStruct((n, hgw), BF16),
        jax.ShapeDtypeStruct((n, d), BF16), jax.ShapeDtypeStruct((n, d), BF16),
    )
    out_specs = (rows(sbw), rows(sbw), rows(sbw), rows(hgk), rows(hgk), rows(hgk),
                 rows(hgw), rows(hgw), rows(d), rows(d))
    return pl.pallas_call(
        functools.partial(_in_proj_kernel, sbw=sbw, hgk=hgk, hgw=hgw, d=d),
        grid=(n // tm,),
        in_specs=[rows(d), const((1, d)), const((d, in_w)), const((1, sbw)), const((1, sbw)),
                  const(lbt.shape), const((sbw, sbw))],
        out_specs=out_specs,
        out_shape=out_shapes,
        compiler_params=pltpu.CompilerParams(
            dimension_semantics=("arbitrary",), vmem_limit_bytes=VMEM_LIMIT),
        name="in_proj",
    )(x2, g1, w_in, qg, kg, lbt, hm)


def _sb_attn_kernel(q_ref, k_ref, v_ref, t_ref, o_ref, sp_ref, z_ref, u_ref, acc_ref, out_ref,
                    *, tq, tk, group):
    i = pl.program_id(2)
    assert tq == 2 * tk
    n_blocks = 2 * i + 2
    lane = lax.broadcasted_iota(jnp.int32, (tk, LANES), 1)
    head0 = lane < SB_HEAD_DIM
    zero = jnp.zeros((tk, LANES), BF16)
    row = lax.broadcasted_iota(jnp.int32, (tq, 2 * tk), 0)
    colk = lax.broadcasted_iota(jnp.int32, (tq, 2 * tk), 1) % tk
    pair_lanes = [slice(g * LANES, (g + 1) * LANES) for g in range(group)]

    def key_start(n):
        return pl.multiple_of((n_blocks - 1 - n) * tk, tk)

    def block_diag(ref, n, psl):
        blk = ref[0, pl.ds(key_start(n), tk), psl]
        return jnp.concatenate([jnp.where(head0, blk, zero), jnp.where(head0, zero, blk)], axis=0)


    def logits_mm(g, blk):
        psl = pair_lanes[g]
        return _dot_nt(q_ref[0, :, psl], block_diag(k_ref, blk[0], psl))

    def logits_fin(g, z, blk, diag_offset):
        sp = jnp.maximum(z, 0.0) + jnp.log(1.0 + jnp.exp2(-jnp.abs(z))) * LOG2E
        if diag_offset is not None:
            sp = jnp.where(colk + diag_offset < row, sp, 0.0)
        sp_ref[blk[1], g] = sp.astype(BF16)
        z_ref[blk[1], g] = z

    def exponent_mm(g, blk):
        return _dot(sp_ref[blk[1], g], t_ref[...])

    def exponent_fin(g, cum, blk, diag_offset):
        total = cum + acc_ref[g]
        u = z_ref[blk[1], g] - total
        if diag_offset is not None:
            u = jnp.where(colk + diag_offset < row, u, -1e30)
        u_ref[blk[1], g] = u
        acc_ref[g] = jnp.concatenate(
            [jnp.broadcast_to(total[:, 0:1], (tq, tk)),
             jnp.broadcast_to(total[:, tk:tk + 1], (tq, tk))], axis=1)

    def weights_mm(g, blk):
        return _dot(jnp.exp2(u_ref[blk[1], g]).astype(BF16),
                    block_diag(v_ref, blk[0], pair_lanes[g]))

    def weights_fin(g, pv):
        out_ref[g] += pv

    def run(slots):
        units = [(slot, g) for slot in slots for g in range(group)]

        def issue(slot, g):
            return (logits_mm(g, slot["logits"][0]) if "logits" in slot else None,
                    exponent_mm(g, slot["exponent"][0]) if "exponent" in slot else None,
                    weights_mm(g, slot["weights"]) if "weights" in slot else None)

        def finish(slot, g, z, cum, pv):
            if "logits" in slot:
                logits_fin(g, z, *slot["logits"])
            if "exponent" in slot:
                exponent_fin(g, cum, *slot["exponent"])
            if "weights" in slot:
                weights_fin(g, pv)

        inflight = issue(*units[0])
        for idx, unit in enumerate(units):
            ready = inflight
            if idx + 1 < len(units):
                inflight = issue(*units[idx + 1])
            finish(*unit, *ready)

    def steady(n, parity):
        return {"logits": ((n, parity), None), "exponent": ((n - 1, 1 - parity), None),
                "weights": (n - 2, parity)}

    acc_ref[...] = jnp.zeros_like(acc_ref)
    out_ref[...] = jnp.zeros_like(out_ref)

    fill = [{"logits": ((0, 0), tk)},
            {"logits": ((1, 1), 0), "exponent": ((0, 0), tk)}]
    last = (n_blocks - 1, 1)
    drain = [{"exponent": (last, None), "weights": (n_blocks - 2, 0)},
             {"weights": last}]

    @pl.when(i == 0)
    def _():
        run(fill + [{"exponent": ((1, 1), 0), "weights": (0, 0)},
                    {"weights": (1, 1)}])

    @pl.when(i > 0)
    def _():
        run(fill + [{"logits": ((2, 0), None), "exponent": ((1, 1), 0), "weights": (0, 0)},
                    steady(3, 1)])
        slot_pairs = i - 1

        def body(it, carry):
            n = 4 + 4 * it
            run([steady(n, 0), steady(n + 1, 1), steady(n + 2, 0), steady(n + 3, 1)])
            return carry

        lax.fori_loop(0, slot_pairs // 2, body, 0)

        @pl.when(slot_pairs % 2 == 1)
        def _():
            run([steady(n_blocks - 2, 0), steady(n_blocks - 1, 1)])

        run(drain)

    for g, psl in enumerate(pair_lanes):
        o_ref[0, :, psl] = out_ref[g].astype(BF16)


def _sb_attention(q, k, v, *, tq, tk, group):
    b, s, w = q.shape
    gw = group * LANES
    assert w % gw == 0
    jj = np.arange(2 * tk)
    tmat = jnp.asarray((jj[:, None] // tk == jj[None, :] // tk) & (jj[:, None] >= jj[None, :]), BF16)
    return pl.pallas_call(
        functools.partial(_sb_attn_kernel, tq=tq, tk=tk, group=group),
        grid=(b, w // gw, s // tq),
        in_specs=[pl.BlockSpec((1, tq, gw), lambda bi, p, i: (bi, i, p)),
                  pl.BlockSpec((1, s, gw), lambda bi, p, i: (bi, 0, p)),
                  pl.BlockSpec((1, s, gw), lambda bi, p, i: (bi, 0, p)),
                  pl.BlockSpec((2 * tk, 2 * tk), lambda bi, p, i: (0, 0))],
        out_specs=pl.BlockSpec((1, tq, gw), lambda bi, p, i: (bi, i, p)),
        out_shape=jax.ShapeDtypeStruct((b, s, w), BF16),
        scratch_shapes=[pltpu.VMEM((2, group, tq, 2 * tk), BF16),
                        pltpu.VMEM((2, group, tq, 2 * tk), F32),
                        pltpu.VMEM((2, group, tq, 2 * tk), F32),
                        pltpu.VMEM((group, tq, 2 * tk), F32),
                        pltpu.VMEM((group, tq, LANES), F32)],
        compiler_params=pltpu.CompilerParams(
            dimension_semantics=("arbitrary", "arbitrary", "arbitrary"),
            vmem_limit_bytes=VMEM_LIMIT),
        name="sb_attention",
    )(q, k, v, tmat)


def _hgrn_tables(c):
    levels = int(np.log2(c))
    assert 2 ** levels == c
    t = np.arange(c)[:, None]
    j = np.arange(c)[None, :]
    blocks = []
    masks = [np.eye(c, dtype=bool)]
    for l in range(1, levels + 1):
        m = 2 ** l
        r = (t // m) * m + m // 2
        upper = (t % m) >= m // 2
        blocks.append(np.where(upper, (j >= r) & (j <= t), (j > t) & (j < r)))
        masks.append((t // m == j // m) & upper & ((j % m) < m // 2))
    blocks.append(j <= t)
    blocks.append(j > t)
    mcat = np.concatenate(blocks, axis=0).astype(np.float32)
    mcat = np.concatenate([mcat, mcat], axis=1)
    masks = np.stack(masks).astype(np.float32)
    masks = np.concatenate([masks, masks], axis=2)
    return jnp.asarray(mcat, BF16), jnp.asarray(masks, F32), levels


def _hgrn_kernel(q_ref, k_ref, lf_ref, v_ref, g_ref, mcat_ref, mask_ref, ng_ref, o_ref,
                 state_ref, *, c, levels, heads, rows):
    @pl.when(pl.program_id(1) == 0)
    def _():
        state_ref[...] = jnp.zeros_like(state_ref)

    dk, dv = HG_KEY_DIM, HG_VAL_DIM
    zero = jnp.zeros((c, dk), BF16)

    def pair_scores(qs, ks):
        rhs = jnp.concatenate([jnp.concatenate([ks[:, :dk], zero], axis=1),
                               jnp.concatenate([zero, ks[:, dk:]], axis=1)], axis=0)
        return _dot_nt(qs, rhs)

    def prepare(r):
        lf = lf_ref[r]
        hi = lf.astype(BF16)
        lo = (lf - hi.astype(F32)).astype(BF16)
        hilo = jnp.concatenate([hi, lo], axis=0)
        ex = jnp.exp(_dot(mcat_ref[...], hilo))
        eb = ex[levels * c:(levels + 1) * c]
        ek = ex[(levels + 1) * c:(levels + 2) * c]
        decay = jnp.exp(_dot_tn(hilo, jnp.ones((2 * c, dv), BF16)))
        scores, qbs, kbs = [], [], []
        for p in range(heads // 2):
            psl = slice(2 * p * dk, (2 * p + 2) * dk)
            q2 = q_ref[r, :, psl].astype(F32)
            k2 = k_ref[r, :, psl].astype(F32)
            sc = [pair_scores(q2.astype(BF16), k2.astype(BF16))]
            for l in range(1, levels + 1):
                e = ex[(l - 1) * c:l * c, psl]
                sc.append(pair_scores((q2 * e).astype(BF16), (k2 * e).astype(BF16)))
            scores.append(sc)
            qbs.append((q2 * eb[:, psl]).astype(BF16))
            kbs.append((k2 * ek[:, psl]).astype(BF16))
        return decay, scores, qbs, kbs

    def finish(r, decay, scores, qbs, kbs):
        for p in range(heads // 2):
            attn = mask_ref[0] * scores[p][0]
            for l in range(1, levels + 1):
                attn = attn + mask_ref[l] * scores[p][l]
            attn = attn.astype(BF16)
            qb, kb = qbs[p], kbs[p]

            for hh in range(2):
                h = 2 * p + hh
                vsl = slice(h * dv, (h + 1) * dv)
                vh = v_ref[r, :, vsl]
                st = state_ref[r, h]
                o = _dot(jnp.concatenate([attn[:, hh * c:(hh + 1) * c], qb[:, hh * dk:(hh + 1) * dk]], axis=1),
                         jnp.concatenate([vh, st.astype(BF16)], axis=0))
                state_ref[r, h] = st * decay[h * dk:(h + 1) * dk] + _dot_tn(kb[:, hh * dk:(hh + 1) * dk], vh)

                ms = jnp.mean(o * o, axis=-1, keepdims=True)
                o = o * lax.rsqrt(ms + EPS) * ng_ref[...]
                o_ref[r, :, vsl] = (o * g_ref[r, :, vsl].astype(F32)).astype(BF16)

    prepared = prepare(0)
    for r in range(rows):
        ready = prepared
        if r + 1 < rows:
            prepared = prepare(r + 1)
        finish(r, *ready)


def _hgrn(hq, hk, hlf, hv, hg, ng, *, c, rows):
    b, s, kw = hq.shape
    vw = hv.shape[2]
    heads = kw // HG_KEY_DIM
    mcat, masks, levels = _hgrn_tables(c)

    def seq(width):
        return pl.BlockSpec((rows, c, width), lambda bi, ci: (bi, ci, 0))

    return pl.pallas_call(
        functools.partial(_hgrn_kernel, c=c, levels=levels, heads=heads, rows=rows),
        grid=(b // rows, s // c),
        in_specs=[seq(kw), seq(kw), seq(kw), seq(vw), seq(vw),
                  pl.BlockSpec(mcat.shape, lambda bi, ci: (0, 0)),
                  pl.BlockSpec(masks.shape, lambda bi, ci: (0, 0, 0)),
                  pl.BlockSpec((1, HG_VAL_DIM), lambda bi, ci: (0, 0))],
        out_specs=seq(vw),
        out_shape=jax.ShapeDtypeStruct((b, s, vw), BF16),
        scratch_shapes=[pltpu.VMEM((rows, heads, HG_KEY_DIM, HG_VAL_DIM), F32)],
        compiler_params=pltpu.CompilerParams(
            dimension_semantics=("arbitrary", "arbitrary"), vmem_limit_bytes=VMEM_LIMIT),
        name="hgrn2",
    )(hq, hk, hlf, hv, hg, mcat, masks, ng)


def _tail_kernel(x_ref, osb_ref, ohg_ref, gsb_ref, ghg_ref, wb_ref, wo_ref, g2_ref,
                 wg_ref, wu_ref, wd_ref, out_ref, *, sbw, ffn_chunks):
    ysb = _dot(osb_ref[...], wb_ref[:sbw, :])
    yhg = _dot(ohg_ref[...], wb_ref[sbw:, :])
    mixed = gsb_ref[...].astype(F32) * ysb + ghg_ref[...].astype(F32) * yhg
    h = x_ref[...] + _dot(mixed.astype(BF16), wo_ref[...])
    ms = jnp.mean(h * h, axis=-1, keepdims=True)
    hn = (h * lax.rsqrt(ms + EPS) * g2_ref[...]).astype(BF16)
    acc = h
    for c0, c1 in ffn_chunks:
        a = _dot(hn, wg_ref[:, c0:c1])
        u = _dot(hn, wu_ref[:, c0:c1])
        acc = acc + _dot((a * _sigmoid(a) * u).astype(BF16), wd_ref[c0:c1, :])
    out_ref[...] = acc


def _tail(x2, osb, ohg, gsb, ghg, wb, wo, g2, wg, wu, wd, *, tm):
    n, d = x2.shape
    sbw = osb.shape[1]
    hgw = ohg.shape[1]
    f = wg.shape[1]
    half = (f // 2) // 256 * 256
    ffn_chunks = ((0, half), (half, f)) if 0 < half < f else ((0, f),)

    def rows(width):
        return pl.BlockSpec((tm, width), lambda i: (i, 0))

    def const(shape):
        return pl.BlockSpec(shape, lambda i: (0, 0), pipeline_mode=pl.Buffered(1))

    return pl.pallas_call(
        functools.partial(_tail_kernel, sbw=sbw, ffn_chunks=ffn_chunks),
        grid=(n // tm,),
        in_specs=[rows(d), rows(sbw), rows(hgw), rows(d), rows(d),
                  const(wb.shape), const(wo.shape), const((1, d)),
                  const(wg.shape), const(wu.shape), const(wd.shape)],
        out_specs=rows(d),
        out_shape=jax.ShapeDtypeStruct((n, d), F32),
        compiler_params=pltpu.CompilerParams(
            dimension_semantics=("arbitrary",), vmem_limit_bytes=VMEM_LIMIT),
        name="tail",
    )(x2, osb, ohg, gsb, ghg, wb, wo, g2, wg, wu, wd)


def _layer(x, norm1_g, w_in, q_norm_g, k_norm_g, lb_table, hg_norm_g, w_branch, w_out,
           norm2_g, w_gate, w_up, w_down, *, tm_in, tq, tk, group, chunk, hg_rows, tm_tail):
    b, s, d = x.shape
    hgk = lb_table.shape[1]
    mix = w_branch.shape[0]
    in_w = w_in.shape[1]
    sbw = in_w - 2 * hgk - 2 * d - 2 * mix
    hgw = mix - sbw
    assert sbw % LANES == 0 and hgk % HG_KEY_DIM == 0 and hgw % HG_VAL_DIM == 0
    sb_heads = sbw // SB_HEAD_DIM

    x2 = x.reshape(b * s, d)
    scale = SB_HEAD_DIM ** -0.5 * LOG2E
    qg = jnp.tile(q_norm_g.astype(F32) * scale, sb_heads).reshape(1, sbw)
    kg = jnp.tile(k_norm_g.astype(F32), sb_heads).reshape(1, sbw)

    q, k, v, hq, hk, hlf, hv, hg, gsb, ghg = _in_proj(
        x2, norm1_g.reshape(1, d), w_in.astype(BF16), qg, kg, lb_table.astype(F32),
        sbw=sbw, hgk=hgk, hgw=hgw, tm=tm_in)

    def seq(t):
        return t.reshape(b, s, t.shape[-1])

    o_sb = _sb_attention(seq(q), seq(k), seq(v), tq=tq, tk=tk, group=group)
    o_hg = _hgrn(seq(hq), seq(hk), seq(hlf), seq(hv), seq(hg),
                 hg_norm_g.reshape(1, HG_VAL_DIM).astype(F32), c=chunk, rows=hg_rows)

    out = _tail(x2, o_sb.reshape(b * s, sbw), o_hg.reshape(b * s, hgw), gsb, ghg,
                w_branch.astype(BF16), w_out.astype(BF16), norm2_g.reshape(1, d),
                w_gate.astype(BF16), w_up.astype(BF16), w_down.astype(BF16), tm=tm_tail)
    return out.reshape(b, s, d)


def kernel(x, norm1_g, w_in, q_norm_g, k_norm_g, lb_table, hg_norm_g, w_branch, w_out,
           norm2_g, w_ffn_gate, w_ffn_up, w_ffn_down):
    depth = w_in.shape[0]
    assert depth == 1 and lb_table.shape[0] == depth + 1
    return _layer(x, norm1_g[0], w_in[0], q_norm_g[0], k_norm_g[0], lb_table, hg_norm_g[0],
                  w_branch[0], w_out[0], norm2_g[0], w_ffn_gate[0], w_ffn_up[0], w_ffn_down[0],
                  tm_in=512, tq=256, tk=128, group=4, chunk=128, hg_rows=2, tm_tail=512)
```

```python
import functools

import numpy as np
import jax
import jax.numpy as jnp
from jax import lax
from jax.experimental import pallas as pl
from jax.experimental.pallas import tpu as pltpu

F32 = jnp.float32
BF16 = jnp.bfloat16
EPS = 1e-6
LOG2E = 1.4426950408889634

SB_HEAD_DIM = 64
HG_KEY_DIM = 128
HG_VAL_DIM = 128
LANES = 128

VMEM_LIMIT = 56 * 1024 * 1024


def _sigmoid(t):
    return 1.0 / (1.0 + jnp.exp(-t))


def _dot(a, b):
    return jnp.dot(a, b, preferred_element_type=F32)


def _dot_nt(a, b):
    return lax.dot_general(a, b, (((1,), (1,)), ((), ())), preferred_element_type=F32)


def _dot_tn(a, b):
    return lax.dot_general(a, b, (((0,), (0,)), ((), ())), preferred_element_type=F32)


def _in_proj_kernel(x_ref, g1_ref, w_ref, qg_ref, kg_ref, lbt_ref, hm_ref,
                    q_ref, k0_ref, k1_ref, v0_ref, v1_ref, hq_ref, hk_ref, hlf_ref, hv_ref, hg_ref,
                    gsb_ref, ghg_ref, *, sbw, hgk, hgw, d):
    x = x_ref[...]
    ms = jnp.mean(x * x, axis=-1, keepdims=True)
    xn = (x * lax.rsqrt(ms + EPS) * g1_ref[...]).astype(BF16)

    col = [0]

    def proj(width):
        c0 = col[0]
        col[0] = c0 + width
        return _dot(xn, w_ref[:, c0:c0 + width])

    def head_norm(t, g):
        hms = _dot((t * t).astype(BF16), hm_ref[...])
        return t * lax.rsqrt(hms + EPS) * g

    q_ref[...] = head_norm(proj(sbw), qg_ref[...]).astype(BF16)
    even_head = (lax.broadcasted_iota(jnp.int32, (x.shape[0], sbw), 1) // SB_HEAD_DIM) % 2 == 0
    for t, t0_ref, t1_ref in ((head_norm(proj(sbw), kg_ref[...]), k0_ref, k1_ref),
                              (proj(sbw), v0_ref, v1_ref)):
        t0_ref[...] = jnp.where(even_head, t, 0.0).astype(BF16)
        t1_ref[...] = jnp.where(even_head, 0.0, t).astype(BF16)

    hq = proj(hgk)
    hq_ref[...] = (hq * _sigmoid(hq)).astype(BF16)

    lbt = lbt_ref[...]
    e = jnp.exp(lbt - jnp.max(lbt, axis=0, keepdims=True))
    lb = e[0:1, :] / jnp.sum(e, axis=0, keepdims=True)
    f = lb + (1.0 - lb) * _sigmoid(proj(hgk))
    hlf_ref[...] = jnp.log(f)
    hk_ref[...] = (1.0 - f).astype(BF16)

    hv_ref[...] = proj(hgw).astype(BF16)
    hg = proj(hgw)
    hg_ref[...] = (hg * _sigmoid(hg)).astype(BF16)
    gsb_ref[...] = _sigmoid(proj(d)).astype(BF16)
    ghg_ref[...] = _sigmoid(proj(d)).astype(BF16)


def _in_proj(x2, g1, w_in, qg, kg, lbt, *, sbw, hgk, hgw, tm):
    n, d = x2.shape
    in_w = w_in.shape[1]
    assert in_w == 3 * sbw + 2 * hgk + 2 * hgw + 2 * d
    heads = np.arange(sbw) // SB_HEAD_DIM
    hm = jnp.asarray((heads[:, None] == heads[None, :]) / SB_HEAD_DIM, BF16)

    def rows(width):
        return pl.BlockSpec((tm, width), lambda i: (i, 0))

    def const(shape):
        return pl.BlockSpec(shape, lambda i: (0, 0), pipeline_mode=pl.Buffered(1))

    out_shapes = (
        jax.ShapeDtypeStruct((n, sbw), BF16), jax.ShapeDtypeStruct((n, sbw), BF16),
        jax.ShapeDtypeStruct((n, sbw), BF16), jax.ShapeDtypeStruct((n, sbw), BF16),
        jax.ShapeDtypeStruct((n, sbw), BF16),
        jax.ShapeDtypeStruct((n, hgk), BF16), jax.ShapeDtypeStruct((n, hgk), BF16),
        jax.ShapeDtypeStruct((n, hgk), F32),
        jax.ShapeDtypeStruct((n, hgw), BF16), jax.ShapeDtypeStruct((n, hgw), BF16),
        jax.ShapeDtypeStruct((n, d), BF16), jax.ShapeDtypeStruct((n, d), BF16),
    )
    out_specs = (rows(sbw), rows(sbw), rows(sbw), rows(sbw), rows(sbw), rows(hgk), rows(hgk), rows(hgk),
                 rows(hgw), rows(hgw), rows(d), rows(d))
    return pl.pallas_call(
        functools.partial(_in_proj_kernel, sbw=sbw, hgk=hgk, hgw=hgw, d=d),
        grid=(n // tm,),
        in_specs=[rows(d), const((1, d)), const((d, in_w)), const((1, sbw)), const((1, sbw)),
                  const(lbt.shape), const((sbw, sbw))],
        out_specs=out_specs,
        out_shape=out_shapes,
        compiler_params=pltpu.CompilerParams(
            dimension_semantics=("arbitrary",), vmem_limit_bytes=VMEM_LIMIT),
        name="in_proj",
    )(x2, g1, w_in, qg, kg, lbt, hm)


def _sb_attn_kernel(q_ref, k0_ref, k1_ref, v0_ref, v1_ref, t_ref, o_ref,
                    sp_ref, z_ref, u_ref, acc_ref, out_ref, *, tq, tk, group):
    i = pl.program_id(2)
    assert tq == 2 * tk
    n_blocks = 2 * i + 2
    row = lax.broadcasted_iota(jnp.int32, (tq, 2 * tk), 0)
    colk = lax.broadcasted_iota(jnp.int32, (tq, 2 * tk), 1) % tk
    pair_lanes = [slice(g * LANES, (g + 1) * LANES) for g in range(group)]

    def causal(diag_offset, rows):
        return (colk + diag_offset < row)[rows]

    def key_start(n):
        return pl.multiple_of((n_blocks - 1 - n) * tk, tk)

    def block_diag(refs, n, psl):
        return jnp.concatenate([r[0, pl.ds(key_start(n), tk), psl] for r in refs], axis=0)


    def logits_mm(g, blk, rows):
        psl = pair_lanes[g]
        return _dot_nt(q_ref[0, rows, psl], block_diag((k0_ref, k1_ref), blk[0], psl))

    def logits_fin(g, z, blk, rows, diag_offset):
        sp = jnp.maximum(z, 0.0) + jnp.log(1.0 + jnp.exp2(-jnp.abs(z))) * LOG2E
        if diag_offset is not None:
            sp = jnp.where(causal(diag_offset, rows), sp, 0.0)
        sp_ref[blk[1], g, rows] = sp.astype(BF16)
        z_ref[blk[1], g, rows] = z

    def exponent_mm(g, blk, rows):
        return _dot(sp_ref[blk[1], g, rows], t_ref[...])

    def exponent_fin(g, cum, blk, rows, diag_offset):
        total = cum + acc_ref[g, rows]
        u = z_ref[blk[1], g, rows] - total
        if diag_offset is not None:
            u = jnp.where(causal(diag_offset, rows), u, -1e30)
        u_ref[blk[1], g, rows] = u
        half = (total.shape[0], tk)
        acc_ref[g, rows] = jnp.concatenate(
            [jnp.broadcast_to(total[:, 0:1], half), jnp.broadcast_to(total[:, tk:tk + 1], half)], axis=1)

    def weights_mm(g, blk, rows):
        return _dot(jnp.exp2(u_ref[blk[1], g, rows]).astype(BF16),
                    block_diag((v0_ref, v1_ref), blk[0], pair_lanes[g]))

    def weights_fin(g, pv, rows):
        out_ref[g, rows] += pv

    def run(slots):
        units = [(slot, g) for slot in slots for g in range(group)]

        def issue(slot, g):
            return (logits_mm(g, *slot["logits"][:2]) if "logits" in slot else None,
                    exponent_mm(g, *slot["exponent"][:2]) if "exponent" in slot else None,
                    weights_mm(g, *slot["weights"]) if "weights" in slot else None)

        def finish(slot, g, z, cum, pv):
            if "logits" in slot:
                logits_fin(g, z, *slot["logits"])
            if "exponent" in slot:
                exponent_fin(g, cum, *slot["exponent"])
            if "weights" in slot:
                weights_fin(g, pv, slot["weights"][1])

        inflight = issue(*units[0])
        for idx, unit in enumerate(units):
            ready = inflight
            if idx + 1 < len(units):
                inflight = issue(*units[idx + 1])
            finish(*unit, *ready)

    every = slice(None)

    def steady(n, parity):
        return {"logits": ((n, parity), every, None), "exponent": ((n - 1, 1 - parity), every, None),
                "weights": ((n - 2, parity), every)}

    acc_ref[...] = jnp.zeros_like(acc_ref)
    out_ref[...] = jnp.zeros_like(out_ref)

    lower = slice(tk, tq)
    fill = [{"logits": ((0, 0), lower, tk)},
            {"logits": ((1, 1), every, 0), "exponent": ((0, 0), lower, tk)}]
    last = (n_blocks - 1, 1)
    drain = [{"exponent": (last, every, None), "weights": ((n_blocks - 2, 0), every)},
             {"weights": (last, every)}]

    @pl.when(i == 0)
    def _():
        run(fill + [{"exponent": ((1, 1), every, 0), "weights": ((0, 0), lower)},
                    {"weights": ((1, 1), every)}])

    @pl.when(i > 0)
    def _():
        run(fill + [{"logits": ((2, 0), every, None), "exponent": ((1, 1), every, 0), "weights": ((0, 0), lower)},
                    steady(3, 1)])
        slot_pairs = i - 1

        def body(it, carry):
            n = 4 + 4 * it
            run([steady(n, 0), steady(n + 1, 1), steady(n + 2, 0), steady(n + 3, 1)])
            return carry

        lax.fori_loop(0, slot_pairs // 2, body, 0)

        @pl.when(slot_pairs % 2 == 1)
        def _():
            run([steady(n_blocks - 2, 0), steady(n_blocks - 1, 1)])

        run(drain)

    for g, psl in enumerate(pair_lanes):
        o_ref[0, :, psl] = out_ref[g].astype(BF16)


def _sb_attention(q, k, v, *, tq, tk, group):
    b, s, w = q.shape
    gw = group * LANES
    assert w % gw == 0
    jj = np.arange(2 * tk)
    tmat = jnp.asarray((jj[:, None] // tk == jj[None, :] // tk) & (jj[:, None] >= jj[None, :]), BF16)

    def keys():
        return pl.BlockSpec((1, s, gw), lambda bi, p, i: (bi, 0, p))

    return pl.pallas_call(
        functools.partial(_sb_attn_kernel, tq=tq, tk=tk, group=group),
        grid=(b, w // gw, s // tq),
        in_specs=[pl.BlockSpec((1, tq, gw), lambda bi, p, i: (bi, i, p)),
                  keys(), keys(), keys(), keys(),
                  pl.BlockSpec((2 * tk, 2 * tk), lambda bi, p, i: (0, 0))],
        out_specs=pl.BlockSpec((1, tq, gw), lambda bi, p, i: (bi, i, p)),
        out_shape=jax.ShapeDtypeStruct((b, s, w), BF16),
        scratch_shapes=[pltpu.VMEM((2, group, tq, 2 * tk), BF16),
                        pltpu.VMEM((2, group, tq, 2 * tk), F32),
                        pltpu.VMEM((2, group, tq, 2 * tk), F32),
                        pltpu.VMEM((group, tq, 2 * tk), F32),
                        pltpu.VMEM((group, tq, LANES), F32)],
        compiler_params=pltpu.CompilerParams(
            dimension_semantics=("arbitrary", "arbitrary", "arbitrary"),
            vmem_limit_bytes=VMEM_LIMIT),
        name="sb_attention",
    )(q, *k, *v, tmat)


def _hgrn_tables(c):
    levels = int(np.log2(c))
    assert 2 ** levels == c
    t = np.arange(c)[:, None]
    j = np.arange(c)[None, :]
    blocks = []
    masks = [np.eye(c, dtype=bool)]
    for l in range(1, levels + 1):
        m = 2 ** l
        r = (t // m) * m + m // 2
        upper = (t % m) >= m // 2
        blocks.append(np.where(upper, (j >= r) & (j <= t), (j > t) & (j < r)))
        masks.append((t // m == j // m) & upper & ((j % m) < m // 2))
    blocks.append(j <= t)
    blocks.append(j > t)
    mcat = np.concatenate(blocks, axis=0).astype(np.float32)
    mcat = np.concatenate([mcat, mcat], axis=1)
    masks = np.stack(masks).astype(np.float32)
    masks = np.concatenate([masks, masks], axis=2)
    return jnp.asarray(mcat, BF16), jnp.asarray(masks, F32), levels


def _hgrn_kernel(q_ref, k_ref, lf_ref, v_ref, g_ref, mcat_ref, mask_ref, ng_ref, o_ref,
                 state_ref, *, c, levels, heads, rows):
    @pl.when(pl.program_id(1) == 0)
    def _():
        state_ref[...] = jnp.zeros_like(state_ref)

    dk, dv = HG_KEY_DIM, HG_VAL_DIM
    zero = jnp.zeros((c, dk), BF16)

    def pair_scores(qs, ks):
        rhs = jnp.concatenate([jnp.concatenate([ks[:, :dk], zero], axis=1),
                               jnp.concatenate([zero, ks[:, dk:]], axis=1)], axis=0)
        return _dot_nt(qs, rhs)

    def prepare(r):
        lf = lf_ref[r]
        hi = lf.astype(BF16)
        lo = (lf - hi.astype(F32)).astype(BF16)
        hilo = jnp.concatenate([hi, lo], axis=0)
        ex = jnp.exp(_dot(mcat_ref[...], hilo))
        eb = ex[levels * c:(levels + 1) * c]
        ek = ex[(levels + 1) * c:(levels + 2) * c]
        decay = jnp.exp(_dot_tn(hilo, jnp.ones((2 * c, dv), BF16)))
        scores, qbs, kbs = [], [], []
        for p in range(heads // 2):
            psl = slice(2 * p * dk, (2 * p + 2) * dk)
            q2 = q_ref[r, :, psl].astype(F32)
            k2 = k_ref[r, :, psl].astype(F32)
            sc = [pair_scores(q2.astype(BF16), k2.astype(BF16))]
            for l in range(1, levels + 1):
                e = ex[(l - 1) * c:l * c, psl]
                sc.append(pair_scores((q2 * e).astype(BF16), (k2 * e).astype(BF16)))
            scores.append(sc)
            qbs.append((q2 * eb[:, psl]).astype(BF16))
            kbs.append((k2 * ek[:, psl]).astype(BF16))
        return decay, scores, qbs, kbs

    def finish(r, decay, scores, qbs, kbs):
        for p in range(heads // 2):
            attn = mask_ref[0] * scores[p][0]
            for l in range(1, levels + 1):
                attn = attn + mask_ref[l] * scores[p][l]
            attn = attn.astype(BF16)
            qb, kb = qbs[p], kbs[p]

            for hh in range(2):
                h = 2 * p + hh
                vsl = slice(h * dv, (h + 1) * dv)
                vh = v_ref[r, :, vsl]
                st = state_ref[r, h]
                o = _dot(jnp.concatenate([attn[:, hh * c:(hh + 1) * c], qb[:, hh * dk:(hh + 1) * dk]], axis=1),
                         jnp.concatenate([vh, st.astype(BF16)], axis=0))
                state_ref[r, h] = st * decay[h * dk:(h + 1) * dk] + _dot_tn(kb[:, hh * dk:(hh + 1) * dk], vh)

                ms = jnp.mean(o * o, axis=-1, keepdims=True)
                o = o * lax.rsqrt(ms + EPS) * ng_ref[...]
                o_ref[r, :, vsl] = (o * g_ref[r, :, vsl].astype(F32)).astype(BF16)

    prepared = prepare(0)
    for r in range(rows):
        ready = prepared
        if r + 1 < rows:
            prepared = prepare(r + 1)
        finish(r, *ready)


def _hgrn(hq, hk, hlf, hv, hg, ng, *, c, rows):
    b, s, kw = hq.shape
    vw = hv.shape[2]
    heads = kw // HG_KEY_DIM
    assert b % rows == 0 and heads % 2 == 0
    mcat, masks, levels = _hgrn_tables(c)

    def seq(width):
        return pl.BlockSpec((rows, c, width), lambda bi, ci: (bi, ci, 0))

    return pl.pallas_call(
        functools.partial(_hgrn_kernel, c=c, levels=levels, heads=heads, rows=rows),
        grid=(b // rows, s // c),
        in_specs=[seq(kw), seq(kw), seq(kw), seq(vw), seq(vw),
                  pl.BlockSpec(mcat.shape, lambda bi, ci: (0, 0)),
                  pl.BlockSpec(masks.shape, lambda bi, ci: (0, 0, 0)),
                  pl.BlockSpec((1, HG_VAL_DIM), lambda bi, ci: (0, 0))],
        out_specs=seq(vw),
        out_shape=jax.ShapeDtypeStruct((b, s, vw), BF16),
        scratch_shapes=[pltpu.VMEM((rows, heads, HG_KEY_DIM, HG_VAL_DIM), F32)],
        compiler_params=pltpu.CompilerParams(
            dimension_semantics=("arbitrary", "arbitrary"), vmem_limit_bytes=VMEM_LIMIT),
        name="hgrn2",
    )(hq, hk, hlf, hv, hg, mcat, masks, ng)


def _tail_kernel(x_ref, osb_ref, ohg_ref, gsb_ref, ghg_ref, wb_ref, wo_ref, g2_ref,
                 wg_ref, wu_ref, wd_ref, out_ref, *, sbw, ffn_chunks):
    ysb = _dot(osb_ref[...], wb_ref[:sbw, :])
    yhg = _dot(ohg_ref[...], wb_ref[sbw:, :])
    mixed = gsb_ref[...].astype(F32) * ysb + ghg_ref[...].astype(F32) * yhg
    h = x_ref[...] + _dot(mixed.astype(BF16), wo_ref[...])
    ms = jnp.mean(h * h, axis=-1, keepdims=True)
    hn = (h * lax.rsqrt(ms + EPS) * g2_ref[...]).astype(BF16)
    acc = h
    for c0, c1 in ffn_chunks:
        a = _dot(hn, wg_ref[:, c0:c1])
        u = _dot(hn, wu_ref[:, c0:c1])
        acc = acc + _dot((a * _sigmoid(a) * u).astype(BF16), wd_ref[c0:c1, :])
    out_ref[...] = acc


def _tail(x2, osb, ohg, gsb, ghg, wb, wo, g2, wg, wu, wd, *, tm):
    n, d = x2.shape
    sbw = osb.shape[1]
    hgw = ohg.shape[1]
    f = wg.shape[1]
    half = (f // 2) // 256 * 256
    ffn_chunks = ((0, half), (half, f)) if 0 < half < f else ((0, f),)

    def rows(width):
        return pl.BlockSpec((tm, width), lambda i: (i, 0))

    def const(shape):
        return pl.BlockSpec(shape, lambda i: (0, 0), pipeline_mode=pl.Buffered(1))

    return pl.pallas_call(
        functools.partial(_tail_kernel, sbw=sbw, ffn_chunks=ffn_chunks),
        grid=(n // tm,),
        in_specs=[rows(d), rows(sbw), rows(hgw), rows(d), rows(d),
                  const(wb.shape), const(wo.shape), const((1, d)),
                  const(wg.shape), const(wu.shape), const(wd.shape)],
        out_specs=rows(d),
        out_shape=jax.ShapeDtypeStruct((n, d), F32),
        compiler_params=pltpu.CompilerParams(
            dimension_semantics=("arbitrary",), vmem_limit_bytes=VMEM_LIMIT),
        name="tail",
    )(x2, osb, ohg, gsb, ghg, wb, wo, g2, wg, wu, wd)


def _layer(x, norm1_g, w_in, q_norm_g, k_norm_g, lb_table, hg_norm_g, w_branch, w_out,
           norm2_g, w_gate, w_up, w_down, *, tm_in, tq, tk, group, chunk, hg_rows, tm_tail):
    b, s, d = x.shape
    hgk = lb_table.shape[1]
    mix = w_branch.shape[0]
    in_w = w_in.shape[1]
    sbw = in_w - 2 * hgk - 2 * d - 2 * mix
    hgw = mix - sbw
    assert sbw % LANES == 0 and hgk % HG_KEY_DIM == 0 and hgw % HG_VAL_DIM == 0
    sb_heads = sbw // SB_HEAD_DIM

    x2 = x.reshape(b * s, d)
    scale = SB_HEAD_DIM ** -0.5 * LOG2E
    qg = jnp.tile(q_norm_g.astype(F32) * scale, sb_heads).reshape(1, sbw)
    kg = jnp.tile(k_norm_g.astype(F32), sb_heads).reshape(1, sbw)

    q, k0, k1, v0, v1, hq, hk, hlf, hv, hg, gsb, ghg = _in_proj(
        x2, norm1_g.reshape(1, d), w_in.astype(BF16), qg, kg, lb_table.astype(F32),
        sbw=sbw, hgk=hgk, hgw=hgw, tm=tm_in)

    def seq(t):
        return t.reshape(b, s, t.shape[-1])

    o_sb = _sb_attention(seq(q), (seq(k0), seq(k1)), (seq(v0), seq(v1)), tq=tq, tk=tk, group=group)
    o_hg = _hgrn(seq(hq), seq(hk), seq(hlf), seq(hv), seq(hg),
                 hg_norm_g.reshape(1, HG_VAL_DIM).astype(F32), c=chunk, rows=hg_rows)

    out = _tail(x2, o_sb.reshape(b * s, sbw), o_hg.reshape(b * s, hgw), gsb, ghg,
                w_branch.astype(BF16), w_out.astype(BF16), norm2_g.reshape(1, d),
                w_gate.astype(BF16), w_up.astype(BF16), w_down.astype(BF16), tm=tm_tail)
    return out.reshape(b, s, d)


def kernel(x, norm1_g, w_in, q_norm_g, k_norm_g, lb_table, hg_norm_g, w_branch, w_out,
           norm2_g, w_ffn_gate, w_ffn_up, w_ffn_down):
    depth = w_in.shape[0]
    assert depth == 1 and lb_table.shape[0] == depth + 1
    return _layer(x, norm1_g[0], w_in[0], q_norm_g[0], k_norm_g[0], lb_table, hg_norm_g[0],
                  w_branch[0], w_out[0], norm2_g[0], w_ffn_gate[0], w_ffn_up[0], w_ffn_down[0],
                  tm_in=1024, tq=256, tk=128, group=4, chunk=128, hg_rows=2, tm_tail=512)
```

```python
import functools

import numpy as np
import jax
import jax.numpy as jnp
from jax import lax
from jax.experimental import pallas as pl
from jax.experimental.pallas import tpu as pltpu

F32 = jnp.float32
BF16 = jnp.bfloat16
EPS = 1e-6
LOG2E = 1.4426950408889634

SB_HEAD_DIM = 64
HG_KEY_DIM = 128
HG_VAL_DIM = 128
LANES = 128

VMEM_LIMIT = 56 * 1024 * 1024


def _sigmoid(t):
    return 1.0 / (1.0 + jnp.exp(-t))


def _dot(a, b):
    return jnp.dot(a, b, preferred_element_type=F32)


def _dot_nt(a, b):
    return lax.dot_general(a, b, (((1,), (1,)), ((), ())), preferred_element_type=F32)


def _dot_tn(a, b):
    return lax.dot_general(a, b, (((0,), (0,)), ((), ())), preferred_element_type=F32)


def _in_proj_kernel(x_ref, g1_ref, w_ref, qg_ref, kg_ref, lbt_ref, hm_ref,
                    q_ref, k0_ref, k1_ref, v0_ref, v1_ref, hq_ref, hk_ref, hlf_ref, hv_ref, hg_ref,
                    gsb_ref, ghg_ref, *, sbw, hgk, hgw, d):
    x = x_ref[...]
    ms = jnp.mean(x * x, axis=-1, keepdims=True)
    xn = (x * lax.rsqrt(ms + EPS) * g1_ref[...]).astype(BF16)

    col = [0]

    def proj(width):
        c0 = col[0]
        col[0] = c0 + width
        return _dot(xn, w_ref[:, c0:c0 + width])

    def head_norm(t, g):
        hms = _dot((t * t).astype(BF16), hm_ref[...])
        return t * lax.rsqrt(hms + EPS) * g

    q_ref[...] = head_norm(proj(sbw), qg_ref[...]).astype(BF16)
    even_head = (lax.broadcasted_iota(jnp.int32, (x.shape[0], sbw), 1) // SB_HEAD_DIM) % 2 == 0
    for t, t0_ref, t1_ref in ((head_norm(proj(sbw), kg_ref[...]), k0_ref, k1_ref),
                              (proj(sbw), v0_ref, v1_ref)):
        t0_ref[...] = jnp.where(even_head, t, 0.0).astype(BF16)
        t1_ref[...] = jnp.where(even_head, 0.0, t).astype(BF16)

    hq = proj(hgk)
    hq_ref[...] = (hq * _sigmoid(hq)).astype(BF16)

    lbt = lbt_ref[...]
    e = jnp.exp(lbt - jnp.max(lbt, axis=0, keepdims=True))
    lb = e[0:1, :] / jnp.sum(e, axis=0, keepdims=True)
    f = lb + (1.0 - lb) * _sigmoid(proj(hgk))
    hlf_ref[...] = jnp.log(f)
    hk_ref[...] = (1.0 - f).astype(BF16)

    hv_ref[...] = proj(hgw).astype(BF16)
    hg = proj(hgw)
    hg_ref[...] = (hg * _sigmoid(hg)).astype(BF16)
    gsb_ref[...] = _sigmoid(proj(d)).astype(BF16)
    ghg_ref[...] = _sigmoid(proj(d)).astype(BF16)


def _in_proj(x2, g1, w_in, qg, kg, lbt, *, sbw, hgk, hgw, tm):
    n, d = x2.shape
    in_w = w_in.shape[1]
    assert in_w == 3 * sbw + 2 * hgk + 2 * hgw + 2 * d
    heads = np.arange(sbw) // SB_HEAD_DIM
    hm = jnp.asarray((heads[:, None] == heads[None, :]) / SB_HEAD_DIM, BF16)

    def rows(width):
        return pl.BlockSpec((tm, width), lambda i: (i, 0))

    def const(shape):
        return pl.BlockSpec(shape, lambda i: (0, 0), pipeline_mode=pl.Buffered(1))

    out_shapes = (
        jax.ShapeDtypeStruct((n, sbw), BF16), jax.ShapeDtypeStruct((n, sbw), BF16),
        jax.ShapeDtypeStruct((n, sbw), BF16), jax.ShapeDtypeStruct((n, sbw), BF16),
        jax.ShapeDtypeStruct((n, sbw), BF16),
        jax.ShapeDtypeStruct((n, hgk), BF16), jax.ShapeDtypeStruct((n, hgk), BF16),
        jax.ShapeDtypeStruct((n, hgk), F32),
        jax.ShapeDtypeStruct((n, hgw), BF16), jax.ShapeDtypeStruct((n, hgw), BF16),
        jax.ShapeDtypeStruct((n, d), BF16), jax.ShapeDtypeStruct((n, d), BF16),
    )
    out_specs = (rows(sbw), rows(sbw), rows(sbw), rows(sbw), rows(sbw), rows(hgk), rows(hgk), rows(hgk),
                 rows(hgw), rows(hgw), rows(d), rows(d))
    return pl.pallas_call(
        functools.partial(_in_proj_kernel, sbw=sbw, hgk=hgk, hgw=hgw, d=d),
        grid=(n // tm,),
        in_specs=[rows(d), const((1, d)), const((d, in_w)), const((1, sbw)), const((1, sbw)),
                  const(lbt.shape), const((sbw, sbw))],
        out_specs=out_specs,
        out_shape=out_shapes,
        compiler_params=pltpu.CompilerParams(
            dimension_semantics=("arbitrary",), vmem_limit_bytes=VMEM_LIMIT),
        name="in_proj",
    )(x2, g1, w_in, qg, kg, lbt, hm)


def _sb_attn_kernel(q_ref, k0_ref, k1_ref, v0_ref, v1_ref, t_ref, o_ref,
                    sp_ref, z_ref, u_ref, acc_ref, out_ref, *, tq, tk, group):
    i = pl.program_id(2)
    assert tq == 2 * tk
    n_blocks = 2 * i + 2
    row = lax.broadcasted_iota(jnp.int32, (tq, 2 * tk), 0)
    colk = lax.broadcasted_iota(jnp.int32, (tq, 2 * tk), 1) % tk
    pair_lanes = [slice(g * LANES, (g + 1) * LANES) for g in range(group)]

    def causal(diag_offset, rows):
        return (colk + diag_offset < row)[rows]

    def key_start(n):
        return pl.multiple_of((n_blocks - 1 - n) * tk, tk)

    def block_diag(refs, n, psl):
        return jnp.concatenate([r[0, pl.ds(key_start(n), tk), psl] for r in refs], axis=0)


    def logits_mm(g, blk, rows):
        psl = pair_lanes[g]
        return _dot_nt(q_ref[0, rows, psl], block_diag((k0_ref, k1_ref), blk[0], psl))

    def logits_fin(g, z, blk, rows, diag_offset):
        sp = jnp.maximum(z, 0.0) + jnp.log(1.0 + jnp.exp2(-jnp.abs(z))) * LOG2E
        if diag_offset is not None:
            sp = jnp.where(causal(diag_offset, rows), sp, 0.0)
        sp_ref[blk[1], g, rows] = sp.astype(BF16)
        z_ref[blk[1], g, rows] = z

    def exponent_mm(g, blk, rows):
        return _dot(sp_ref[blk[1], g, rows], t_ref[...])

    def exponent_fin(g, cum, blk, rows, diag_offset):
        total = cum + acc_ref[g, rows]
        u = z_ref[blk[1], g, rows] - total
        if diag_offset is not None:
            u = jnp.where(causal(diag_offset, rows), u, -1e30)
        u_ref[blk[1], g, rows] = u
        half = (total.shape[0], tk)
        acc_ref[g, rows] = jnp.concatenate(
            [jnp.broadcast_to(total[:, 0:1], half), jnp.broadcast_to(total[:, tk:tk + 1], half)], axis=1)

    def weights_mm(g, blk, rows):
        return _dot(jnp.exp2(u_ref[blk[1], g, rows]).astype(BF16),
                    block_diag((v0_ref, v1_ref), blk[0], pair_lanes[g]))

    def weights_fin(g, pv, rows):
        out_ref[g, rows] += pv

    def run(slots):
        units = [(slot, g) for slot in slots for g in range(group)]

        def issue(slot, g):
            return (logits_mm(g, *slot["logits"][:2]) if "logits" in slot else None,
                    exponent_mm(g, *slot["exponent"][:2]) if "exponent" in slot else None,
                    weights_mm(g, *slot["weights"]) if "weights" in slot else None)

        def finish(slot, g, z, cum, pv):
            if "logits" in slot:
                logits_fin(g, z, *slot["logits"])
            if "exponent" in slot:
                exponent_fin(g, cum, *slot["exponent"])
            if "weights" in slot:
                weights_fin(g, pv, slot["weights"][1])

        inflight = issue(*units[0])
        for idx, unit in enumerate(units):
            ready = inflight
            if idx + 1 < len(units):
                inflight = issue(*units[idx + 1])
            finish(*unit, *ready)

    every = slice(None)

    def steady(n, parity):
        return {"logits": ((n, parity), every, None), "exponent": ((n - 1, 1 - parity), every, None),
                "weights": ((n - 2, parity), every)}

    acc_ref[...] = jnp.zeros_like(acc_ref)
    out_ref[...] = jnp.zeros_like(out_ref)

    lower = slice(tk, tq)
    fill = [{"logits": ((0, 0), lower, tk)},
            {"logits": ((1, 1), every, 0), "exponent": ((0, 0), lower, tk)}]
    last = (n_blocks - 1, 1)
    drain = [{"exponent": (last, every, None), "weights": ((n_blocks - 2, 0), every)},
             {"weights": (last, every)}]

    @pl.when(i == 0)
    def _():
        run(fill + [{"exponent": ((1, 1), every, 0), "weights": ((0, 0), lower)},
                    {"weights": ((1, 1), every)}])

    @pl.when(i > 0)
    def _():
        run(fill + [{"logits": ((2, 0), every, None), "exponent": ((1, 1), every, 0), "weights": ((0, 0), lower)},
                    steady(3, 1)])
        slot_pairs = i - 1

        def body(it, carry):
            n = 4 + 8 * it
            run([steady(n + d, d % 2) for d in range(8)])
            return carry

        lax.fori_loop(0, slot_pairs // 4, body, 0)
        done = 4 + 8 * (slot_pairs // 4)

        @pl.when(slot_pairs % 4 >= 2)
        def _():
            run([steady(done + d, d % 2) for d in range(4)])

        @pl.when(slot_pairs % 2 == 1)
        def _():
            run([steady(n_blocks - 2, 0), steady(n_blocks - 1, 1)])

        run(drain)

    for g, psl in enumerate(pair_lanes):
        o_ref[0, :, psl] = out_ref[g].astype(BF16)


def _sb_attention(q, k, v, *, tq, tk, group):
    b, s, w = q.shape
    gw = group * LANES
    assert w % gw == 0
    jj = np.arange(2 * tk)
    tmat = jnp.asarray((jj[:, None] // tk == jj[None, :] // tk) & (jj[:, None] >= jj[None, :]), BF16)

    def keys():
        return pl.BlockSpec((1, s, gw), lambda bi, p, i: (bi, 0, p))

    return pl.pallas_call(
        functools.partial(_sb_attn_kernel, tq=tq, tk=tk, group=group),
        grid=(b, w // gw, s // tq),
        in_specs=[pl.BlockSpec((1, tq, gw), lambda bi, p, i: (bi, i, p)),
                  keys(), keys(), keys(), keys(),
                  pl.BlockSpec((2 * tk, 2 * tk), lambda bi, p, i: (0, 0))],
        out_specs=pl.BlockSpec((1, tq, gw), lambda bi, p, i: (bi, i, p)),
        out_shape=jax.ShapeDtypeStruct((b, s, w), BF16),
        scratch_shapes=[pltpu.VMEM((2, group, tq, 2 * tk), BF16),
                        pltpu.VMEM((2, group, tq, 2 * tk), F32),
                        pltpu.VMEM((2, group, tq, 2 * tk), F32),
                        pltpu.VMEM((group, tq, 2 * tk), F32),
                        pltpu.VMEM((group, tq, LANES), F32)],
        compiler_params=pltpu.CompilerParams(
            dimension_semantics=("arbitrary", "arbitrary", "arbitrary"),
            vmem_limit_bytes=VMEM_LIMIT),
        name="sb_attention",
    )(q, *k, *v, tmat)


def _hgrn_tables(c):
    levels = int(np.log2(c))
    assert 2 ** levels == c
    t = np.arange(c)[:, None]
    j = np.arange(c)[None, :]
    blocks = []
    masks = [np.eye(c, dtype=bool)]
    for l in range(1, levels + 1):
        m = 2 ** l
        r = (t // m) * m + m // 2
        upper = (t % m) >= m // 2
        blocks.append(np.where(upper, (j >= r) & (j <= t), (j > t) & (j < r)))
        masks.append((t // m == j // m) & upper & ((j % m) < m // 2))
    blocks.append(j <= t)
    blocks.append(j > t)
    mcat = np.concatenate(blocks, axis=0).astype(np.float32)
    mcat = np.concatenate([mcat, mcat], axis=1)
    masks = np.stack(masks).astype(np.float32)
    masks = np.concatenate([masks, masks], axis=2)
    return jnp.asarray(mcat, BF16), jnp.asarray(masks, F32), levels


def _hgrn_kernel(q_ref, k_ref, lf_ref, v_ref, g_ref, mcat_ref, mask_ref, ng_ref, o_ref,
                 state_ref, *, c, levels, heads, rows):
    @pl.when(pl.program_id(1) == 0)
    def _():
        state_ref[...] = jnp.zeros_like(state_ref)

    dk, dv = HG_KEY_DIM, HG_VAL_DIM
    zero = jnp.zeros((c, dk), BF16)

    def pair_scores(qs, ks):
        rhs = jnp.concatenate([jnp.concatenate([ks[:, :dk], zero], axis=1),
                               jnp.concatenate([zero, ks[:, dk:]], axis=1)], axis=0)
        return _dot_nt(qs, rhs)

    def prepare(r):
        lf = lf_ref[r]
        hi = lf.astype(BF16)
        lo = (lf - hi.astype(F32)).astype(BF16)
        hilo = jnp.concatenate([hi, lo], axis=0)
        ex = jnp.exp(_dot(mcat_ref[...], hilo))
        eb = ex[levels * c:(levels + 1) * c]
        ek = ex[(levels + 1) * c:(levels + 2) * c]
        decay = jnp.exp(_dot_tn(hilo, jnp.ones((2 * c, dv), BF16)))
        scores, qbs, kbs = [], [], []
        for p in range(heads // 2):
            psl = slice(2 * p * dk, (2 * p + 2) * dk)
            q2 = q_ref[r, :, psl].astype(F32)
            k2 = k_ref[r, :, psl].astype(F32)
            sc = [pair_scores(q2.astype(BF16), k2.astype(BF16))]
            for l in range(1, levels + 1):
                e = ex[(l - 1) * c:l * c, psl]
                sc.append(pair_scores((q2 * e).astype(BF16), (k2 * e).astype(BF16)))
            scores.append(sc)
            qbs.append((q2 * eb[:, psl]).astype(BF16))
            kbs.append((k2 * ek[:, psl]).astype(BF16))
        return decay, scores, qbs, kbs

    def finish(r, decay, scores, qbs, kbs):
        for p in range(heads // 2):
            attn = mask_ref[0] * scores[p][0]
            for l in range(1, levels + 1):
                attn = attn + mask_ref[l] * scores[p][l]
            attn = attn.astype(BF16)
            qb, kb = qbs[p], kbs[p]

            for hh in range(2):
                h = 2 * p + hh
                vsl = slice(h * dv, (h + 1) * dv)
                vh = v_ref[r, :, vsl]
                st = state_ref[r, h]
                o = _dot(jnp.concatenate([attn[:, hh * c:(hh + 1) * c], qb[:, hh * dk:(hh + 1) * dk]], axis=1),
                         jnp.concatenate([vh, st.astype(BF16)], axis=0))
                state_ref[r, h] = st * decay[h * dk:(h + 1) * dk] + _dot_tn(kb[:, hh * dk:(hh + 1) * dk], vh)

                ms = jnp.mean(o * o, axis=-1, keepdims=True)
                o = o * lax.rsqrt(ms + EPS) * ng_ref[...]
                o_ref[r, :, vsl] = (o * g_ref[r, :, vsl].astype(F32)).astype(BF16)

    prepared = prepare(0)
    for r in range(rows):
        ready = prepared
        if r + 1 < rows:
            prepared = prepare(r + 1)
        finish(r, *ready)


def _hgrn(hq, hk, hlf, hv, hg, ng, *, c, rows):
    b, s, kw = hq.shape
    vw = hv.shape[2]
    heads = kw // HG_KEY_DIM
    assert b % rows == 0 and heads % 2 == 0
    mcat, masks, levels = _hgrn_tables(c)

    def seq(width):
        return pl.BlockSpec((rows, c, width), lambda bi, ci: (bi, ci, 0))

    return pl.pallas_call(
        functools.partial(_hgrn_kernel, c=c, levels=levels, heads=heads, rows=rows),
        grid=(b // rows, s // c),
        in_specs=[seq(kw), seq(kw), seq(kw), seq(vw), seq(vw),
                  pl.BlockSpec(mcat.shape, lambda bi, ci: (0, 0)),
                  pl.BlockSpec(masks.shape, lambda bi, ci: (0, 0, 0)),
                  pl.BlockSpec((1, HG_VAL_DIM), lambda bi, ci: (0, 0))],
        out_specs=seq(vw),
        out_shape=jax.ShapeDtypeStruct((b, s, vw), BF16),
        scratch_shapes=[pltpu.VMEM((rows, heads, HG_KEY_DIM, HG_VAL_DIM), F32)],
        compiler_params=pltpu.CompilerParams(
            dimension_semantics=("arbitrary", "arbitrary"), vmem_limit_bytes=VMEM_LIMIT),
        name="hgrn2",
    )(hq, hk, hlf, hv, hg, mcat, masks, ng)


def _tail_kernel(x_ref, osb_ref, ohg_ref, gsb_ref, ghg_ref, wb_ref, wo_ref, g2_ref,
                 wg_ref, wu_ref, wd_ref, out_ref, *, sbw, ffn_chunks):
    ysb = _dot(osb_ref[...], wb_ref[:sbw, :])
    yhg = _dot(ohg_ref[...], wb_ref[sbw:, :])
    mixed = gsb_ref[...].astype(F32) * ysb + ghg_ref[...].astype(F32) * yhg
    h = x_ref[...] + _dot(mixed.astype(BF16), wo_ref[...])
    ms = jnp.mean(h * h, axis=-1, keepdims=True)
    hn = (h * lax.rsqrt(ms + EPS) * g2_ref[...]).astype(BF16)
    acc = h
    for c0, c1 in ffn_chunks:
        a = _dot(hn, wg_ref[:, c0:c1])
        u = _dot(hn, wu_ref[:, c0:c1])
        acc = acc + _dot((a * _sigmoid(a) * u).astype(BF16), wd_ref[c0:c1, :])
    out_ref[...] = acc


def _tail(x2, osb, ohg, gsb, ghg, wb, wo, g2, wg, wu, wd, *, tm):
    n, d = x2.shape
    sbw = osb.shape[1]
    hgw = ohg.shape[1]
    f = wg.shape[1]
    half = (f // 2) // 256 * 256
    ffn_chunks = ((0, half), (half, f)) if 0 < half < f else ((0, f),)

    def rows(width):
        return pl.BlockSpec((tm, width), lambda i: (i, 0))

    def const(shape):
        return pl.BlockSpec(shape, lambda i: (0, 0), pipeline_mode=pl.Buffered(1))

    return pl.pallas_call(
        functools.partial(_tail_kernel, sbw=sbw, ffn_chunks=ffn_chunks),
        grid=(n // tm,),
        in_specs=[rows(d), rows(sbw), rows(hgw), rows(d), rows(d),
                  const(wb.shape), const(wo.shape), const((1, d)),
                  const(wg.shape), const(wu.shape), const(wd.shape)],
        out_specs=rows(d),
        out_shape=jax.ShapeDtypeStruct((n, d), F32),
        compiler_params=pltpu.CompilerParams(
            dimension_semantics=("arbitrary",), vmem_limit_bytes=VMEM_LIMIT),
        name="tail",
    )(x2, osb, ohg, gsb, ghg, wb, wo, g2, wg, wu, wd)


def _layer(x, norm1_g, w_in, q_norm_g, k_norm_g, lb_table, hg_norm_g, w_branch, w_out,
           norm2_g, w_gate, w_up, w_down, *, tm_in, tq, tk, group, chunk, hg_rows, tm_tail):
    b, s, d = x.shape
    hgk = lb_table.shape[1]
    mix = w_branch.shape[0]
    in_w = w_in.shape[1]
    sbw = in_w - 2 * hgk - 2 * d - 2 * mix
    hgw = mix - sbw
    assert sbw % LANES == 0 and hgk % HG_KEY_DIM == 0 and hgw % HG_VAL_DIM == 0
    sb_heads = sbw // SB_HEAD_DIM

    x2 = x.reshape(b * s, d)
    scale = SB_HEAD_DIM ** -0.5 * LOG2E
    qg = jnp.tile(q_norm_g.astype(F32) * scale, sb_heads).reshape(1, sbw)
    kg = jnp.tile(k_norm_g.astype(F32), sb_heads).reshape(1, sbw)

    q, k0, k1, v0, v1, hq, hk, hlf, hv, hg, gsb, ghg = _in_proj(
        x2, norm1_g.reshape(1, d), w_in.astype(BF16), qg, kg, lb_table.astype(F32),
        sbw=sbw, hgk=hgk, hgw=hgw, tm=tm_in)

    def seq(t):
        return t.reshape(b, s, t.shape[-1])

    o_sb = _sb_attention(seq(q), (seq(k0), seq(k1)), (seq(v0), seq(v1)), tq=tq, tk=tk, group=group)
    o_hg = _hgrn(seq(hq), seq(hk), seq(hlf), seq(hv), seq(hg),
                 hg_norm_g.reshape(1, HG_VAL_DIM).astype(F32), c=chunk, rows=hg_rows)

    out = _tail(x2, o_sb.reshape(b * s, sbw), o_hg.reshape(b * s, hgw), gsb, ghg,
                w_branch.astype(BF16), w_out.astype(BF16), norm2_g.reshape(1, d),
                w_gate.astype(BF16), w_up.astype(BF16), w_down.astype(BF16), tm=tm_tail)
    return out.reshape(b, s, d)


def kernel(x, norm1_g, w_in, q_norm_g, k_norm_g, lb_table, hg_norm_g, w_branch, w_out,
           norm2_g, w_ffn_gate, w_ffn_up, w_ffn_down):
    depth = w_in.shape[0]
    assert depth == 1 and lb_table.shape[0] == depth + 1
    return _layer(x, norm1_g[0], w_in[0], q_norm_g[0], k_norm_g[0], lb_table, hg_norm_g[0],
                  w_branch[0], w_out[0], norm2_g[0], w_ffn_gate[0], w_ffn_up[0], w_ffn_down[0],
                  tm_in=1024, tq=256, tk=128, group=4, chunk=128, hg_rows=4, tm_tail=512)
```

```python
import functools

import numpy as np
import jax
import jax.numpy as jnp
from jax import lax
from jax.experimental import pallas as pl
from jax.experimental.pallas import tpu as pltpu

F32 = jnp.float32
BF16 = jnp.bfloat16
EPS = 1e-6
LOG2E = 1.4426950408889634

SB_HEAD_DIM = 64
HG_KEY_DIM = 128
HG_VAL_DIM = 128
LANES = 128
MXU_TILE = 256
MASKED = -1e30

VMEM_LIMIT = 56 * 1024 * 1024


def _sigmoid(t):
    return 1.0 / (1.0 + jnp.exp(-t))


def _dot(a, b):
    return jnp.dot(a, b, preferred_element_type=F32)


def _dot_nt(a, b):
    return lax.dot_general(a, b, (((1,), (1,)), ((), ())), preferred_element_type=F32)


def _dot_tn(a, b):
    return lax.dot_general(a, b, (((0,), (0,)), ((), ())), preferred_element_type=F32)


def _in_proj_kernel(x_ref, g1_ref, w_ref, qg_ref, kg_ref, lbt_ref, hm_ref,
                    q_ref, k0_ref, k1_ref, v0_ref, v1_ref, hact_ref, hlf_ref, gate_ref,
                    *, sbw, hgk, hgw, d):
    x = x_ref[...]
    ms = jnp.mean(x * x, axis=-1, keepdims=True)
    xn = (x * lax.rsqrt(ms + EPS) * g1_ref[...]).astype(BF16)

    col = [0]

    def proj(width):
        c0 = col[0]
        col[0] = c0 + width
        return _dot(xn, w_ref[:, c0:c0 + width])

    def head_norm(t, g):
        hms = _dot((t * t).astype(BF16), hm_ref[...])
        return t * lax.rsqrt(hms + EPS) * g

    q_ref[...] = head_norm(proj(sbw), qg_ref[...]).astype(BF16)
    even_head = (lax.broadcasted_iota(jnp.int32, (x.shape[0], sbw), 1) // SB_HEAD_DIM) % 2 == 0
    for t, t0_ref, t1_ref in ((head_norm(proj(sbw), kg_ref[...]), k0_ref, k1_ref),
                              (proj(sbw), v0_ref, v1_ref)):
        t0_ref[...] = jnp.where(even_head, t, 0.0).astype(BF16)
        t1_ref[...] = jnp.where(even_head, 0.0, t).astype(BF16)

    hq = proj(hgk)
    hact_ref[:, :hgk] = (hq * _sigmoid(hq)).astype(BF16)

    lbt = lbt_ref[...]
    e = jnp.exp(lbt - jnp.max(lbt, axis=0, keepdims=True))
    lb = e[0:1, :] / jnp.sum(e, axis=0, keepdims=True)
    f = lb + (1.0 - lb) * _sigmoid(proj(hgk))
    hlf_ref[...] = jnp.log(f)
    hact_ref[:, hgk:2 * hgk] = (1.0 - f).astype(BF16)

    hact_ref[:, 2 * hgk:2 * hgk + hgw] = proj(hgw).astype(BF16)
    hg = proj(hgw)
    hact_ref[:, 2 * hgk + hgw:] = (hg * _sigmoid(hg)).astype(BF16)
    gate_ref[:, :d] = _sigmoid(proj(d)).astype(BF16)
    gate_ref[:, d:] = _sigmoid(proj(d)).astype(BF16)


def _in_proj(x2, g1, w_in, qg, kg, lbt, *, sbw, hgk, hgw, tm):
    n, d = x2.shape
    in_w = w_in.shape[1]
    assert in_w == 3 * sbw + 2 * hgk + 2 * hgw + 2 * d
    heads = np.arange(sbw) // SB_HEAD_DIM
    hm = jnp.asarray((heads[:, None] == heads[None, :]) / SB_HEAD_DIM, BF16)

    def rows(width):
        return pl.BlockSpec((tm, width), lambda i: (i, 0))

    def const(shape):
        return pl.BlockSpec(shape, lambda i: (0, 0), pipeline_mode=pl.Buffered(1))

    widths = (sbw, sbw, sbw, sbw, sbw, 2 * hgk + 2 * hgw, hgk, 2 * d)
    dtypes = (BF16, BF16, BF16, BF16, BF16, BF16, F32, BF16)
    out_shapes = tuple(jax.ShapeDtypeStruct((n, w), t) for w, t in zip(widths, dtypes))
    out_specs = tuple(rows(w) for w in widths)
    return pl.pallas_call(
        functools.partial(_in_proj_kernel, sbw=sbw, hgk=hgk, hgw=hgw, d=d),
        grid=(n // tm,),
        in_specs=[rows(d), const((1, d)), const((d, in_w)), const((1, sbw)), const((1, sbw)),
                  const(lbt.shape), const((sbw, sbw))],
        out_specs=out_specs,
        out_shape=out_shapes,
        compiler_params=pltpu.CompilerParams(
            dimension_semantics=("arbitrary",), vmem_limit_bytes=VMEM_LIMIT),
        name="in_proj",
    )(x2, g1, w_in, qg, kg, lbt, hm)


def _sb_attn_kernel(q_ref, k0_ref, k1_ref, v0_ref, v1_ref, t_ref, o_ref,
                    sp_ref, z_ref, u_ref, acc_ref, out_ref, *, tq, tk, group):
    i = pl.program_id(2)
    assert tq == 2 * tk
    n_blocks = 2 * i + 2
    row = lax.broadcasted_iota(jnp.int32, (tq, 2 * tk), 0)
    colk = lax.broadcasted_iota(jnp.int32, (tq, 2 * tk), 1) % tk
    pair_lanes = [slice(g * LANES, (g + 1) * LANES) for g in range(group)]

    def causal(diag_offset, rows):
        return (colk + diag_offset < row)[rows]

    def key_start(n):
        return pl.multiple_of((n_blocks - 1 - n) * tk, tk)

    def block_diag(refs, n, psl):
        return jnp.concatenate([r[0, pl.ds(key_start(n), tk), psl] for r in refs], axis=0)


    def logits_mm(g, blk, rows):
        psl = pair_lanes[g]
        return _dot_nt(q_ref[0, rows, psl], block_diag((k0_ref, k1_ref), blk[0], psl))

    def logits_fin(g, z, blk, rows, diag_offset):
        sp = jnp.maximum(z, 0.0) + jnp.log(1.0 + jnp.exp2(-jnp.abs(z))) * LOG2E
        if diag_offset is not None:
            sp = jnp.where(causal(diag_offset, rows), sp, 0.0)
        sp_ref[blk[1], g, rows] = sp.astype(BF16)
        z_ref[blk[1], g, rows] = z

    def exponent_mm(g, blk, rows):
        return _dot(sp_ref[blk[1], g, rows], t_ref[...])

    def exponent_fin(g, cum, blk, rows, diag_offset):
        total = cum + acc_ref[g, rows]
        u = z_ref[blk[1], g, rows] - total
        if diag_offset is not None:
            u = jnp.where(causal(diag_offset, rows), u, MASKED)
        u_ref[blk[1], g, rows] = u
        half = (total.shape[0], tk)
        acc_ref[g, rows] = jnp.concatenate(
            [jnp.broadcast_to(total[:, 0:1], half), jnp.broadcast_to(total[:, tk:tk + 1], half)], axis=1)

    def weights_mm(g, blk, rows):
        return _dot(jnp.exp2(u_ref[blk[1], g, rows]).astype(BF16),
                    block_diag((v0_ref, v1_ref), blk[0], pair_lanes[g]))

    def weights_fin(g, pv, rows):
        out_ref[g, rows] += pv

    def run(slots):
        units = [(slot, g) for slot in slots for g in range(group)]

        def issue(slot, g):
            return (logits_mm(g, *slot["logits"][:2]) if "logits" in slot else None,
                    exponent_mm(g, *slot["exponent"][:2]) if "exponent" in slot else None,
                    weights_mm(g, *slot["weights"]) if "weights" in slot else None)

        def finish(slot, g, z, cum, pv):
            if "logits" in slot:
                logits_fin(g, z, *slot["logits"])
            if "exponent" in slot:
                exponent_fin(g, cum, *slot["exponent"])
            if "weights" in slot:
                weights_fin(g, pv, slot["weights"][1])

        inflight = issue(*units[0])
        for idx, unit in enumerate(units):
            ready = inflight
            if idx + 1 < len(units):
                inflight = issue(*units[idx + 1])
            finish(*unit, *ready)

    every = slice(None)

    def steady(n, parity):
        return {"logits": ((n, parity), every, None), "exponent": ((n - 1, 1 - parity), every, None),
                "weights": ((n - 2, parity), every)}

    acc_ref[...] = jnp.zeros_like(acc_ref)
    out_ref[...] = jnp.zeros_like(out_ref)

    lower = slice(tk, tq)
    fill = [{"logits": ((0, 0), lower, tk)},
            {"logits": ((1, 1), every, 0), "exponent": ((0, 0), lower, tk)}]
    last = (n_blocks - 1, 1)
    drain = [{"exponent": (last, every, None), "weights": ((n_blocks - 2, 0), every)},
             {"weights": (last, every)}]

    @pl.when(i == 0)
    def _():
        run(fill + [{"exponent": ((1, 1), every, 0), "weights": ((0, 0), lower)},
                    {"weights": ((1, 1), every)}])

    @pl.when(i > 0)
    def _():
        run(fill + [{"logits": ((2, 0), every, None), "exponent": ((1, 1), every, 0), "weights": ((0, 0), lower)},
                    steady(3, 1)])
        slot_pairs = i - 1

        def body(it, carry):
            n = 4 + 8 * it
            run([steady(n + d, d % 2) for d in range(8)])
            return carry

        lax.fori_loop(0, slot_pairs // 4, body, 0)
        done = 4 + 8 * (slot_pairs // 4)

        @pl.when(slot_pairs % 4 >= 2)
        def _():
            run([steady(done + d, d % 2) for d in range(4)])

        @pl.when(slot_pairs % 2 == 1)
        def _():
            run([steady(n_blocks - 2, 0), steady(n_blocks - 1, 1)])

        run(drain)

    for g, psl in enumerate(pair_lanes):
        o_ref[0, :, psl] = out_ref[g].astype(BF16)


def _sb_attention(q, k, v, *, tq, tk, group):
    b, s, w = q.shape
    gw = group * LANES
    assert w % gw == 0
    jj = np.arange(2 * tk)
    tmat = jnp.asarray((jj[:, None] // tk == jj[None, :] // tk) & (jj[:, None] >= jj[None, :]), BF16)

    def keys():
        return pl.BlockSpec((1, s, gw), lambda bi, p, i: (bi, 0, p))

    return pl.pallas_call(
        functools.partial(_sb_attn_kernel, tq=tq, tk=tk, group=group),
        grid=(b, w // gw, s // tq),
        in_specs=[pl.BlockSpec((1, tq, gw), lambda bi, p, i: (bi, i, p)),
                  keys(), keys(), keys(), keys(),
                  pl.BlockSpec((2 * tk, 2 * tk), lambda bi, p, i: (0, 0))],
        out_specs=pl.BlockSpec((1, tq, gw), lambda bi, p, i: (bi, i, p)),
        out_shape=jax.ShapeDtypeStruct((b, s, w), BF16),
        scratch_shapes=[pltpu.VMEM((2, group, tq, 2 * tk), BF16),
                        pltpu.VMEM((2, group, tq, 2 * tk), F32),
                        pltpu.VMEM((2, group, tq, 2 * tk), F32),
                        pltpu.VMEM((group, tq, 2 * tk), F32),
                        pltpu.VMEM((group, tq, LANES), F32)],
        compiler_params=pltpu.CompilerParams(
            dimension_semantics=("arbitrary", "arbitrary", "arbitrary"),
            vmem_limit_bytes=VMEM_LIMIT),
        name="sb_attention",
    )(q, *k, *v, tmat)


def _hgrn_tables(c):
    levels = int(np.log2(c))
    assert 2 ** levels == c
    t = np.arange(c)[:, None]
    j = np.arange(c)[None, :]
    blocks = []
    masks = [np.eye(c, dtype=bool)]
    for l in range(1, levels + 1):
        m = 2 ** l
        r = (t // m) * m + m // 2
        upper = (t % m) >= m // 2
        blocks.append(np.where(upper, (j >= r) & (j <= t), (j > t) & (j < r)))
        masks.append((t // m == j // m) & upper & ((j % m) < m // 2))
    blocks.append(j <= t)
    blocks.append(j > t)
    mcat = np.concatenate(blocks, axis=0).astype(np.float32)
    mcat = np.concatenate([mcat, mcat], axis=1)
    masks = np.stack(masks).astype(np.float32)
    masks = np.concatenate([masks, masks], axis=2)
    return jnp.asarray(mcat, BF16), jnp.asarray(masks, F32), levels


def _hgrn_kernel(a_ref, lf_ref, mcat_ref, mask_ref, ng_ref, o_ref,
                 state_ref, *, c, levels, heads, rows):
    @pl.when(pl.program_id(1) == 0)
    def _():
        state_ref[...] = jnp.zeros_like(state_ref)

    dk, dv = HG_KEY_DIM, HG_VAL_DIM
    kw, vw = heads * dk, heads * dv
    zero = jnp.zeros((c, dk), BF16)

    def pair_scores(qs, ks):
        rhs = jnp.concatenate([jnp.concatenate([ks[:, :dk], zero], axis=1),
                               jnp.concatenate([zero, ks[:, dk:]], axis=1)], axis=0)
        return _dot_nt(qs, rhs)

    def prepare(r):
        lf = lf_ref[r]
        hi = lf.astype(BF16)
        lo = (lf - hi.astype(F32)).astype(BF16)
        hilo = jnp.concatenate([hi, lo], axis=0)
        ex = jnp.exp(_dot(mcat_ref[...], hilo))
        eb = ex[levels * c:(levels + 1) * c]
        ek = ex[(levels + 1) * c:(levels + 2) * c]
        decay = jnp.exp(_dot_tn(hilo, jnp.ones((2 * c, dv), BF16)))
        scores, qbs, kbs = [], [], []
        for p in range(heads // 2):
            psl = slice(2 * p * dk, (2 * p + 2) * dk)
            q2 = a_ref[r, :, psl].astype(F32)
            k2 = a_ref[r, :, kw + psl.start:kw + psl.stop].astype(F32)
            sc = [pair_scores(q2.astype(BF16), k2.astype(BF16))]
            for l in range(1, levels + 1):
                e = ex[(l - 1) * c:l * c, psl]
                sc.append(pair_scores((q2 * e).astype(BF16), (k2 * e).astype(BF16)))
            scores.append(sc)
            qbs.append((q2 * eb[:, psl]).astype(BF16))
            kbs.append((k2 * ek[:, psl]).astype(BF16))
        return decay, scores, qbs, kbs

    def finish(r, decay, scores, qbs, kbs):
        for p in range(heads // 2):
            attn = mask_ref[0] * scores[p][0]
            for l in range(1, levels + 1):
                attn = attn + mask_ref[l] * scores[p][l]
            attn = attn.astype(BF16)
            qb, kb = qbs[p], kbs[p]

            for hh in range(2):
                h = 2 * p + hh
                vsl = slice(h * dv, (h + 1) * dv)
                vh = a_ref[r, :, 2 * kw + vsl.start:2 * kw + vsl.stop]
                st = state_ref[r, h]
                o = _dot(jnp.concatenate([attn[:, hh * c:(hh + 1) * c], qb[:, hh * dk:(hh + 1) * dk]], axis=1),
                         jnp.concatenate([vh, st.astype(BF16)], axis=0))
                state_ref[r, h] = st * decay[h * dk:(h + 1) * dk] + _dot_tn(kb[:, hh * dk:(hh + 1) * dk], vh)

                ms = jnp.mean(o * o, axis=-1, keepdims=True)
                o = o * lax.rsqrt(ms + EPS) * ng_ref[...]
                gate = a_ref[r, :, 2 * kw + vw + vsl.start:2 * kw + vw + vsl.stop]
                o_ref[r, :, vsl] = (o * gate.astype(F32)).astype(BF16)

    prepared = prepare(0)
    for r in range(rows):
        ready = prepared
        if r + 1 < rows:
            prepared = prepare(r + 1)
        finish(r, *ready)


def _hgrn(hact, hlf, ng, *, kw, vw, c, rows):
    b, s, _ = hact.shape
    heads = kw // HG_KEY_DIM
    assert b % rows == 0 and heads % 2 == 0
    mcat, masks, levels = _hgrn_tables(c)

    def seq(width):
        return pl.BlockSpec((rows, c, width), lambda bi, ci: (bi, ci, 0))

    return pl.pallas_call(
        functools.partial(_hgrn_kernel, c=c, levels=levels, heads=heads, rows=rows),
        grid=(b // rows, s // c),
        in_specs=[seq(2 * kw + 2 * vw), seq(kw),
                  pl.BlockSpec(mcat.shape, lambda bi, ci: (0, 0)),
                  pl.BlockSpec(masks.shape, lambda bi, ci: (0, 0, 0)),
                  pl.BlockSpec((1, HG_VAL_DIM), lambda bi, ci: (0, 0))],
        out_specs=seq(vw),
        out_shape=jax.ShapeDtypeStruct((b, s, vw), BF16),
        scratch_shapes=[pltpu.VMEM((rows, heads, HG_KEY_DIM, HG_VAL_DIM), F32)],
        compiler_params=pltpu.CompilerParams(
            dimension_semantics=("arbitrary", "arbitrary"), vmem_limit_bytes=VMEM_LIMIT),
        name="hgrn2",
    )(hact, hlf, mcat, masks, ng)


def _tail_kernel(x_ref, osb_ref, ohg_ref, gate_ref, wb_ref, wo_ref, g2_ref,
                 wg_ref, wu_ref, wd_ref, out_ref, *, sbw, ffn_chunks):
    ysb = _dot(osb_ref[...], wb_ref[:sbw, :])
    yhg = _dot(ohg_ref[...], wb_ref[sbw:, :])
    d = x_ref.shape[1]
    mixed = gate_ref[:, :d].astype(F32) * ysb + gate_ref[:, d:].astype(F32) * yhg
    h = x_ref[...] + _dot(mixed.astype(BF16), wo_ref[...])
    ms = jnp.mean(h * h, axis=-1, keepdims=True)
    hn = (h * lax.rsqrt(ms + EPS) * g2_ref[...]).astype(BF16)
    acc = h
    for c0, c1 in ffn_chunks:
        a = _dot(hn, wg_ref[:, c0:c1])
        u = _dot(hn, wu_ref[:, c0:c1])
        acc = acc + _dot((a * _sigmoid(a) * u).astype(BF16), wd_ref[c0:c1, :])
    out_ref[...] = acc


def _tail(x2, osb, ohg, gates, wb, wo, g2, wg, wu, wd, *, tm):
    n, d = x2.shape
    sbw = osb.shape[1]
    hgw = ohg.shape[1]
    f = wg.shape[1]
    half = (f // 2) // MXU_TILE * MXU_TILE
    ffn_chunks = ((0, half), (half, f)) if 0 < half < f else ((0, f),)

    def rows(width):
        return pl.BlockSpec((tm, width), lambda i: (i, 0))

    def const(shape):
        return pl.BlockSpec(shape, lambda i: (0, 0), pipeline_mode=pl.Buffered(1))

    return pl.pallas_call(
        functools.partial(_tail_kernel, sbw=sbw, ffn_chunks=ffn_chunks),
        grid=(n // tm,),
        in_specs=[rows(d), rows(sbw), rows(hgw), rows(2 * d),
                  const(wb.shape), const(wo.shape), const((1, d)),
                  const(wg.shape), const(wu.shape), const(wd.shape)],
        out_specs=rows(d),
        out_shape=jax.ShapeDtypeStruct((n, d), F32),
        compiler_params=pltpu.CompilerParams(
            dimension_semantics=("arbitrary",), vmem_limit_bytes=VMEM_LIMIT),
        name="tail",
    )(x2, osb, ohg, gates, wb, wo, g2, wg, wu, wd)


def _layer(x, norm1_g, w_in, q_norm_g, k_norm_g, lb_table, hg_norm_g, w_branch, w_out,
           norm2_g, w_gate, w_up, w_down, *, tm_in, tq, tk, group, chunk, hg_rows, tm_tail):
    b, s, d = x.shape
    hgk = lb_table.shape[1]
    mix = w_branch.shape[0]
    in_w = w_in.shape[1]
    sbw = in_w - 2 * hgk - 2 * d - 2 * mix
    hgw = mix - sbw
    assert sbw % LANES == 0 and hgk % HG_KEY_DIM == 0 and hgw % HG_VAL_DIM == 0
    sb_heads = sbw // SB_HEAD_DIM

    x2 = x.reshape(b * s, d)
    scale = SB_HEAD_DIM ** -0.5 * LOG2E
    qg = jnp.tile(q_norm_g.astype(F32) * scale, sb_heads).reshape(1, sbw)
    kg = jnp.tile(k_norm_g.astype(F32), sb_heads).reshape(1, sbw)

    q, k0, k1, v0, v1, hact, hlf, gates = _in_proj(
        x2, norm1_g.reshape(1, d), w_in.astype(BF16), qg, kg, lb_table.astype(F32),
        sbw=sbw, hgk=hgk, hgw=hgw, tm=tm_in)

    def seq(t):
        return t.reshape(b, s, t.shape[-1])

    o_sb = _sb_attention(seq(q), (seq(k0), seq(k1)), (seq(v0), seq(v1)), tq=tq, tk=tk, group=group)
    o_hg = _hgrn(seq(hact), seq(hlf), hg_norm_g.reshape(1, HG_VAL_DIM).astype(F32),
                 kw=hgk, vw=hgw, c=chunk, rows=hg_rows)

    out = _tail(x2, o_sb.reshape(b * s, sbw), o_hg.reshape(b * s, hgw), gates,
                w_branch.astype(BF16), w_out.astype(BF16), norm2_g.reshape(1, d),
                w_gate.astype(BF16), w_up.astype(BF16), w_down.astype(BF16), tm=tm_tail)
    return out.reshape(b, s, d)


def kernel(x, norm1_g, w_in, q_norm_g, k_norm_g, lb_table, hg_norm_g, w_branch, w_out,
           norm2_g, w_ffn_gate, w_ffn_up, w_ffn_down):
    depth = w_in.shape[0]
    assert depth == 1 and lb_table.shape[0] == depth + 1
    return _layer(x, norm1_g[0], w_in[0], q_norm_g[0], k_norm_g[0], lb_table, hg_norm_g[0],
                  w_branch[0], w_out[0], norm2_g[0], w_ffn_gate[0], w_ffn_up[0], w_ffn_down[0],
                  tm_in=1024, tq=256, tk=128, group=4, chunk=128, hg_rows=8, tm_tail=512)
```

```python
import functools

import numpy as np
import jax
import jax.numpy as jnp
from jax import lax
from jax.experimental import pallas as pl
from jax.experimental.pallas import tpu as pltpu

F32 = jnp.float32
BF16 = jnp.bfloat16
EPS = 1e-6
LOG2E = 1.4426950408889634

SB_HEAD_DIM = 64
HG_KEY_DIM = 128
HG_VAL_DIM = 128
LANES = 128
MXU_TILE = 256
MASKED = -1e30

VMEM_LIMIT = 56 * 1024 * 1024


def _sigmoid(t):
    return 1.0 / (1.0 + jnp.exp(-t))


def _dot(a, b):
    return jnp.dot(a, b, preferred_element_type=F32)


def _dot_nt(a, b):
    return lax.dot_general(a, b, (((1,), (1,)), ((), ())), preferred_element_type=F32)


def _dot_tn(a, b):
    return lax.dot_general(a, b, (((0,), (0,)), ((), ())), preferred_element_type=F32)


def _in_proj_kernel(x_ref, g1_ref, w_ref, qg_ref, kg_ref, lbt_ref, hm_ref,
                    q_ref, k0_ref, k1_ref, v0_ref, v1_ref, hact_ref, hlf_ref, gate_ref,
                    *, sbw, hgk, hgw, d):
    x = x_ref[...]
    ms = jnp.mean(x * x, axis=-1, keepdims=True)
    xn = (x * lax.rsqrt(ms + EPS) * g1_ref[...]).astype(BF16)

    col = [0]

    def proj(width):
        c0 = col[0]
        col[0] = c0 + width
        return _dot(xn, w_ref[:, c0:c0 + width])

    def head_norm(t, g):
        hms = _dot((t * t).astype(BF16), hm_ref[...])
        return t * lax.rsqrt(hms + EPS) * g

    q_ref[...] = head_norm(proj(sbw), qg_ref[...]).astype(BF16)
    even_head = (lax.broadcasted_iota(jnp.int32, (x.shape[0], sbw), 1) // SB_HEAD_DIM) % 2 == 0
    for t, t0_ref, t1_ref in ((head_norm(proj(sbw), kg_ref[...]), k0_ref, k1_ref),
                              (proj(sbw), v0_ref, v1_ref)):
        t0_ref[...] = jnp.where(even_head, t, 0.0).astype(BF16)
        t1_ref[...] = jnp.where(even_head, 0.0, t).astype(BF16)

    hq = proj(hgk)
    hact_ref[:, :hgk] = (hq * _sigmoid(hq)).astype(BF16)

    lbt = lbt_ref[...]
    e = jnp.exp(lbt - jnp.max(lbt, axis=0, keepdims=True))
    lb = e[0:1, :] / jnp.sum(e, axis=0, keepdims=True)
    f = lb + (1.0 - lb) * _sigmoid(proj(hgk))
    hlf_ref[...] = jnp.log(f)
    hact_ref[:, hgk:2 * hgk] = (1.0 - f).astype(BF16)

    hact_ref[:, 2 * hgk:2 * hgk + hgw] = proj(hgw).astype(BF16)
    hg = proj(hgw)
    hact_ref[:, 2 * hgk + hgw:] = (hg * _sigmoid(hg)).astype(BF16)
    gate_ref[:, :d] = _sigmoid(proj(d)).astype(BF16)
    gate_ref[:, d:] = _sigmoid(proj(d)).astype(BF16)


def _in_proj(x2, g1, w_in, qg, kg, lbt, *, sbw, hgk, hgw, tm):
    n, d = x2.shape
    in_w = w_in.shape[1]
    assert in_w == 3 * sbw + 2 * hgk + 2 * hgw + 2 * d
    heads = np.arange(sbw) // SB_HEAD_DIM
    hm = jnp.asarray((heads[:, None] == heads[None, :]) / SB_HEAD_DIM, BF16)

    def rows(width):
        return pl.BlockSpec((tm, width), lambda i: (i, 0))

    def const(shape):
        return pl.BlockSpec(shape, lambda i: (0, 0), pipeline_mode=pl.Buffered(1))

    widths = (sbw, sbw, sbw, sbw, sbw, 2 * hgk + 2 * hgw, hgk, 2 * d)
    dtypes = (BF16, BF16, BF16, BF16, BF16, BF16, F32, BF16)
    out_shapes = tuple(jax.ShapeDtypeStruct((n, w), t) for w, t in zip(widths, dtypes))
    out_specs = tuple(rows(w) for w in widths)
    return pl.pallas_call(
        functools.partial(_in_proj_kernel, sbw=sbw, hgk=hgk, hgw=hgw, d=d),
        grid=(n // tm,),
        in_specs=[rows(d), const((1, d)), const((d, in_w)), const((1, sbw)), const((1, sbw)),
                  const(lbt.shape), const((sbw, sbw))],
        out_specs=out_specs,
        out_shape=out_shapes,
        compiler_params=pltpu.CompilerParams(
            dimension_semantics=("arbitrary",), vmem_limit_bytes=VMEM_LIMIT),
        name="in_proj",
    )(x2, g1, w_in, qg, kg, lbt, hm)


def _sb_query_block(i, q_start, q_ref, k0_ref, k1_ref, v0_ref, v1_ref, t_ref, o_ref,
                    sp_ref, z_ref, u_ref, acc_ref, out_ref, *, tq, tk, group):
    assert tq == 2 * tk
    n_blocks = 2 * i + 2
    row = lax.broadcasted_iota(jnp.int32, (tq, 2 * tk), 0)
    colk = lax.broadcasted_iota(jnp.int32, (tq, 2 * tk), 1) % tk
    pair_lanes = [slice(g * LANES, (g + 1) * LANES) for g in range(group)]

    def causal(diag_offset, rows):
        return (colk + diag_offset < row)[rows]

    def key_start(n):
        return pl.multiple_of((n_blocks - 1 - n) * tk, tk)

    def block_diag(refs, n, psl):
        return jnp.concatenate([r[0, pl.ds(key_start(n), tk), psl] for r in refs], axis=0)


    def logits_mm(g, blk, rows):
        psl = pair_lanes[g]
        start = rows.start or 0
        size = (tq if rows.stop is None else rows.stop) - start
        return _dot_nt(q_ref[0, pl.ds(q_start + start, size), psl],
                       block_diag((k0_ref, k1_ref), blk[0], psl))

    def logits_fin(g, z, blk, rows, diag_offset):
        sp = jnp.maximum(z, 0.0) + jnp.log(1.0 + jnp.exp2(-jnp.abs(z))) * LOG2E
        if diag_offset is not None:
            sp = jnp.where(causal(diag_offset, rows), sp, 0.0)
        sp_ref[blk[1], g, rows] = sp.astype(BF16)
        z_ref[blk[1], g, rows] = z

    def exponent_mm(g, blk, rows):
        return _dot(sp_ref[blk[1], g, rows], t_ref[...])

    def exponent_fin(g, cum, blk, rows, diag_offset):
        total = cum + acc_ref[g, rows]
        u = z_ref[blk[1], g, rows] - total
        if diag_offset is not None:
            u = jnp.where(causal(diag_offset, rows), u, MASKED)
        u_ref[blk[1], g, rows] = u
        half = (total.shape[0], tk)
        acc_ref[g, rows] = jnp.concatenate(
            [jnp.broadcast_to(total[:, 0:1], half), jnp.broadcast_to(total[:, tk:tk + 1], half)], axis=1)

    def weights_mm(g, blk, rows):
        return _dot(jnp.exp2(u_ref[blk[1], g, rows]).astype(BF16),
                    block_diag((v0_ref, v1_ref), blk[0], pair_lanes[g]))

    def weights_fin(g, pv, rows):
        out_ref[g, rows] += pv

    def run(slots):
        units = [(slot, g) for slot in slots for g in range(group)]

        def issue(slot, g):
            return (logits_mm(g, *slot["logits"][:2]) if "logits" in slot else None,
                    exponent_mm(g, *slot["exponent"][:2]) if "exponent" in slot else None,
                    weights_mm(g, *slot["weights"]) if "weights" in slot else None)

        def finish(slot, g, z, cum, pv):
            if "logits" in slot:
                logits_fin(g, z, *slot["logits"])
            if "exponent" in slot:
                exponent_fin(g, cum, *slot["exponent"])
            if "weights" in slot:
                weights_fin(g, pv, slot["weights"][1])

        inflight = issue(*units[0])
        for idx, unit in enumerate(units):
            ready = inflight
            if idx + 1 < len(units):
                inflight = issue(*units[idx + 1])
            finish(*unit, *ready)

    every = slice(None)

    def steady(n, parity):
        return {"logits": ((n, parity), every, None), "exponent": ((n - 1, 1 - parity), every, None),
                "weights": ((n - 2, parity), every)}

    acc_ref[...] = jnp.zeros_like(acc_ref)
    out_ref[...] = jnp.zeros_like(out_ref)

    lower = slice(tk, tq)
    fill = [{"logits": ((0, 0), lower, tk)},
            {"logits": ((1, 1), every, 0), "exponent": ((0, 0), lower, tk)}]
    last = (n_blocks - 1, 1)
    drain = [{"exponent": (last, every, None), "weights": ((n_blocks - 2, 0), every)},
             {"weights": (last, every)}]

    @pl.when(i == 0)
    def _():
        run(fill + [{"exponent": ((1, 1), every, 0), "weights": ((0, 0), lower)},
                    {"weights": ((1, 1), every)}])

    @pl.when(i > 0)
    def _():
        run(fill + [{"logits": ((2, 0), every, None), "exponent": ((1, 1), every, 0), "weights": ((0, 0), lower)},
                    steady(3, 1)])
        slot_pairs = i - 1

        def body(it, carry):
            n = 4 + 8 * it
            run([steady(n + d, d % 2) for d in range(8)])
            return carry

        lax.fori_loop(0, slot_pairs // 4, body, 0)
        done = 4 + 8 * (slot_pairs // 4)

        @pl.when(slot_pairs % 4 >= 2)
        def _():
            run([steady(done + d, d % 2) for d in range(4)])

        @pl.when(slot_pairs % 2 == 1)
        def _():
            run([steady(n_blocks - 2, 0), steady(n_blocks - 1, 1)])

        run(drain)

    for g, psl in enumerate(pair_lanes):
        o_ref[0, pl.ds(q_start, tq), psl] = out_ref[g].astype(BF16)


def _sb_attn_kernel(*refs, tq, tk, group, q_per_step):
    def query_block(j, carry):
        _sb_query_block(pl.program_id(2) * q_per_step + j, pl.multiple_of(j * tq, tq), *refs,
                        tq=tq, tk=tk, group=group)
        return carry

    lax.fori_loop(0, q_per_step, query_block, 0)


def _sb_attention(q, k, v, *, tq, tk, group, q_per_step):
    b, s, w = q.shape
    gw = group * LANES
    assert w % gw == 0
    jj = np.arange(2 * tk)
    tmat = jnp.asarray((jj[:, None] // tk == jj[None, :] // tk) & (jj[:, None] >= jj[None, :]), BF16)

    def keys():
        return pl.BlockSpec((1, s, gw), lambda bi, p, i: (bi, 0, p))

    return pl.pallas_call(
        functools.partial(_sb_attn_kernel, tq=tq, tk=tk, group=group, q_per_step=q_per_step),
        grid=(b, w // gw, s // (q_per_step * tq)),
        in_specs=[pl.BlockSpec((1, q_per_step * tq, gw), lambda bi, p, i: (bi, i, p)),
                  keys(), keys(), keys(), keys(),
                  pl.BlockSpec((2 * tk, 2 * tk), lambda bi, p, i: (0, 0))],
        out_specs=pl.BlockSpec((1, q_per_step * tq, gw), lambda bi, p, i: (bi, i, p)),
        out_shape=jax.ShapeDtypeStruct((b, s, w), BF16),
        scratch_shapes=[pltpu.VMEM((2, group, tq, 2 * tk), BF16),
                        pltpu.VMEM((2, group, tq, 2 * tk), F32),
                        pltpu.VMEM((2, group, tq, 2 * tk), F32),
                        pltpu.VMEM((group, tq, 2 * tk), F32),
                        pltpu.VMEM((group, tq, LANES), F32)],
        compiler_params=pltpu.CompilerParams(
            dimension_semantics=("arbitrary", "arbitrary", "arbitrary"),
            vmem_limit_bytes=VMEM_LIMIT),
        name="sb_attention",
    )(q, *k, *v, tmat)


def _hgrn_tables(c):
    levels = int(np.log2(c))
    assert 2 ** levels == c
    t = np.arange(c)[:, None]
    j = np.arange(c)[None, :]
    blocks = []
    masks = [np.eye(c, dtype=bool)]
    for l in range(1, levels + 1):
        m = 2 ** l
        r = (t // m) * m + m // 2
        upper = (t % m) >= m // 2
        blocks.append(np.where(upper, (j >= r) & (j <= t), (j > t) & (j < r)))
        masks.append((t // m == j // m) & upper & ((j % m) < m // 2))
    blocks.append(j <= t)
    blocks.append(j > t)
    mcat = np.concatenate(blocks, axis=0).astype(np.float32)
    mcat = np.concatenate([mcat, mcat], axis=1)
    masks = np.stack(masks).astype(np.float32)
    masks = np.concatenate([masks, masks], axis=2)
    return jnp.asarray(mcat, BF16), jnp.asarray(masks, F32), levels


def _hgrn_kernel(a_ref, lf_ref, mcat_ref, mask_ref, ng_ref, o_ref,
                 state_ref, *, c, levels, heads, rows):
    @pl.when(pl.program_id(1) == 0)
    def _():
        state_ref[...] = jnp.zeros_like(state_ref)

    dk, dv = HG_KEY_DIM, HG_VAL_DIM
    kw, vw = heads * dk, heads * dv
    zero = jnp.zeros((c, dk), BF16)

    def pair_scores(qs, ks):
        rhs = jnp.concatenate([jnp.concatenate([ks[:, :dk], zero], axis=1),
                               jnp.concatenate([zero, ks[:, dk:]], axis=1)], axis=0)
        return _dot_nt(qs, rhs)

    def prepare(r):
        lf = lf_ref[r]
        hi = lf.astype(BF16)
        lo = (lf - hi.astype(F32)).astype(BF16)
        hilo = jnp.concatenate([hi, lo], axis=0)
        ex = jnp.exp(_dot(mcat_ref[...], hilo))
        eb = ex[levels * c:(levels + 1) * c]
        ek = ex[(levels + 1) * c:(levels + 2) * c]
        decay = jnp.exp(_dot_tn(hilo, jnp.ones((2 * c, dv), BF16)))
        scores, qbs, kbs = [], [], []
        for p in range(heads // 2):
            psl = slice(2 * p * dk, (2 * p + 2) * dk)
            q2 = a_ref[r, :, psl].astype(F32)
            k2 = a_ref[r, :, kw + psl.start:kw + psl.stop].astype(F32)
            sc = [pair_scores(q2.astype(BF16), k2.astype(BF16))]
            for l in range(1, levels + 1):
                e = ex[(l - 1) * c:l * c, psl]
                sc.append(pair_scores((q2 * e).astype(BF16), (k2 * e).astype(BF16)))
            scores.append(sc)
            qbs.append((q2 * eb[:, psl]).astype(BF16))
            kbs.append((k2 * ek[:, psl]).astype(BF16))
        return decay, scores, qbs, kbs

    def finish(r, decay, scores, qbs, kbs):
        for p in range(heads // 2):
            attn = mask_ref[0] * scores[p][0]
            for l in range(1, levels + 1):
                attn = attn + mask_ref[l] * scores[p][l]
            attn = attn.astype(BF16)
            qb, kb = qbs[p], kbs[p]

            for hh in range(2):
                h = 2 * p + hh
                vsl = slice(h * dv, (h + 1) * dv)
                vh = a_ref[r, :, 2 * kw + vsl.start:2 * kw + vsl.stop]
                st = state_ref[r, h]
                o = _dot(jnp.concatenate([attn[:, hh * c:(hh + 1) * c], qb[:, hh * dk:(hh + 1) * dk]], axis=1),
                         jnp.concatenate([vh, st.astype(BF16)], axis=0))
                state_ref[r, h] = st * decay[h * dk:(h + 1) * dk] + _dot_tn(kb[:, hh * dk:(hh + 1) * dk], vh)

                ms = jnp.mean(o * o, axis=-1, keepdims=True)
                o = o * lax.rsqrt(ms + EPS) * ng_ref[...]
                gate = a_ref[r, :, 2 * kw + vw + vsl.start:2 * kw + vw + vsl.stop]
                o_ref[r, :, vsl] = (o * gate.astype(F32)).astype(BF16)

    prepared = prepare(0)
    for r in range(rows):
        ready = prepared
        if r + 1 < rows:
            prepared = prepare(r + 1)
        finish(r, *ready)


def _hgrn(hact, hlf, ng, *, kw, vw, c, rows):
    b, s, _ = hact.shape
    heads = kw // HG_KEY_DIM
    assert b % rows == 0 and heads % 2 == 0
    mcat, masks, levels = _hgrn_tables(c)

    def seq(width):
        return pl.BlockSpec((rows, c, width), lambda bi, ci: (bi, ci, 0))

    return pl.pallas_call(
        functools.partial(_hgrn_kernel, c=c, levels=levels, heads=heads, rows=rows),
        grid=(b // rows, s // c),
        in_specs=[seq(2 * kw + 2 * vw), seq(kw),
                  pl.BlockSpec(mcat.shape, lambda bi, ci: (0, 0)),
                  pl.BlockSpec(masks.shape, lambda bi, ci: (0, 0, 0)),
                  pl.BlockSpec((1, HG_VAL_DIM), lambda bi, ci: (0, 0))],
        out_specs=seq(vw),
        out_shape=jax.ShapeDtypeStruct((b, s, vw), BF16),
        scratch_shapes=[pltpu.VMEM((rows, heads, HG_KEY_DIM, HG_VAL_DIM), F32)],
        compiler_params=pltpu.CompilerParams(
            dimension_semantics=("arbitrary", "arbitrary"), vmem_limit_bytes=VMEM_LIMIT),
        name="hgrn2",
    )(hact, hlf, mcat, masks, ng)


def _tail_kernel(x_ref, osb_ref, ohg_ref, gate_ref, wb_ref, wo_ref, g2_ref,
                 wg_ref, wu_ref, wd_ref, out_ref, *, sbw, ffn_chunks):
    ysb = _dot(osb_ref[...], wb_ref[:sbw, :])
    yhg = _dot(ohg_ref[...], wb_ref[sbw:, :])
    d = x_ref.shape[1]
    mixed = gate_ref[:, :d].astype(F32) * ysb + gate_ref[:, d:].astype(F32) * yhg
    h = x_ref[...] + _dot(mixed.astype(BF16), wo_ref[...])
    ms = jnp.mean(h * h, axis=-1, keepdims=True)
    hn = (h * lax.rsqrt(ms + EPS) * g2_ref[...]).astype(BF16)
    acc = h
    for c0, c1 in ffn_chunks:
        a = _dot(hn, wg_ref[:, c0:c1])
        u = _dot(hn, wu_ref[:, c0:c1])
        acc = acc + _dot((a * _sigmoid(a) * u).astype(BF16), wd_ref[c0:c1, :])
    out_ref[...] = acc


def _tail(x2, osb, ohg, gates, wb, wo, g2, wg, wu, wd, *, tm):
    n, d = x2.shape
    sbw = osb.shape[1]
    hgw = ohg.shape[1]
    f = wg.shape[1]
    half = (f // 2) // MXU_TILE * MXU_TILE
    ffn_chunks = ((0, half), (half, f)) if 0 < half < f else ((0, f),)

    def rows(width):
        return pl.BlockSpec((tm, width), lambda i: (i, 0))

    def const(shape):
        return pl.BlockSpec(shape, lambda i: (0, 0), pipeline_mode=pl.Buffered(1))

    return pl.pallas_call(
        functools.partial(_tail_kernel, sbw=sbw, ffn_chunks=ffn_chunks),
        grid=(n // tm,),
        in_specs=[rows(d), rows(sbw), rows(hgw), rows(2 * d),
                  const(wb.shape), const(wo.shape), const((1, d)),
                  const(wg.shape), const(wu.shape), const(wd.shape)],
        out_specs=rows(d),
        out_shape=jax.ShapeDtypeStruct((n, d), F32),
        compiler_params=pltpu.CompilerParams(
            dimension_semantics=("arbitrary",), vmem_limit_bytes=VMEM_LIMIT),
        name="tail",
    )(x2, osb, ohg, gates, wb, wo, g2, wg, wu, wd)


def _layer(x, norm1_g, w_in, q_norm_g, k_norm_g, lb_table, hg_norm_g, w_branch, w_out,
           norm2_g, w_gate, w_up, w_down, *, tm_in, tq, tk, group, q_per_step, chunk, hg_rows, tm_tail):
    b, s, d = x.shape
    hgk = lb_table.shape[1]
    mix = w_branch.shape[0]
    in_w = w_in.shape[1]
    sbw = in_w - 2 * hgk - 2 * d - 2 * mix
    hgw = mix - sbw
    assert sbw % LANES == 0 and hgk % HG_KEY_DIM == 0 and hgw % HG_VAL_DIM == 0
    sb_heads = sbw // SB_HEAD_DIM

    x2 = x.reshape(b * s, d)
    scale = SB_HEAD_DIM ** -0.5 * LOG2E
    qg = jnp.tile(q_norm_g.astype(F32) * scale, sb_heads).reshape(1, sbw)
    kg = jnp.tile(k_norm_g.astype(F32), sb_heads).reshape(1, sbw)

    q, k0, k1, v0, v1, hact, hlf, gates = _in_proj(
        x2, norm1_g.reshape(1, d), w_in.astype(BF16), qg, kg, lb_table.astype(F32),
        sbw=sbw, hgk=hgk, hgw=hgw, tm=tm_in)

    def seq(t):
        return t.reshape(b, s, t.shape[-1])

    o_sb = _sb_attention(seq(q), (seq(k0), seq(k1)), (seq(v0), seq(v1)),
                         tq=tq, tk=tk, group=group, q_per_step=q_per_step)
    o_hg = _hgrn(seq(hact), seq(hlf), hg_norm_g.reshape(1, HG_VAL_DIM).astype(F32),
                 kw=hgk, vw=hgw, c=chunk, rows=hg_rows)

    out = _tail(x2, o_sb.reshape(b * s, sbw), o_hg.reshape(b * s, hgw), gates,
                w_branch.astype(BF16), w_out.astype(BF16), norm2_g.reshape(1, d),
                w_gate.astype(BF16), w_up.astype(BF16), w_down.astype(BF16), tm=tm_tail)
    return out.reshape(b, s, d)


def kernel(x, norm1_g, w_in, q_norm_g, k_norm_g, lb_table, hg_norm_g, w_branch, w_out,
           norm2_g, w_ffn_gate, w_ffn_up, w_ffn_down):
    depth = w_in.shape[0]
    assert depth == 1 and lb_table.shape[0] == depth + 1
    return _layer(x, norm1_g[0], w_in[0], q_norm_g[0], k_norm_g[0], lb_table, hg_norm_g[0],
                  w_branch[0], w_out[0], norm2_g[0], w_ffn_gate[0], w_ffn_up[0], w_ffn_down[0],
                  tm_in=1024, tq=256, tk=128, group=4, q_per_step=4,
                  chunk=128, hg_rows=8, tm_tail=512)
```

```python
import functools

import numpy as np
import jax
import jax.numpy as jnp
from jax import lax
from jax.experimental import pallas as pl
from jax.experimental.pallas import tpu as pltpu

F32 = jnp.float32
BF16 = jnp.bfloat16
EPS = 1e-6
LOG2E = 1.4426950408889634

SB_HEAD_DIM = 64
HG_KEY_DIM = 128
HG_VAL_DIM = 128
LANES = 128
MXU_TILE = 256
MASKED = -1e30

VMEM_LIMIT = 56 * 1024 * 1024


def _sigmoid(t):
    return 1.0 / (1.0 + jnp.exp(-t))


def _dot(a, b):
    return jnp.dot(a, b, preferred_element_type=F32)


def _dot_nt(a, b):
    return lax.dot_general(a, b, (((1,), (1,)), ((), ())), preferred_element_type=F32)


def _dot_tn(a, b):
    return lax.dot_general(a, b, (((0,), (0,)), ((), ())), preferred_element_type=F32)


def _in_proj_kernel(x_ref, g1_ref, w_ref, qg_ref, kg_ref, lbt_ref, hm_ref,
                    q_ref, k0_ref, k1_ref, v0_ref, v1_ref, hact_ref, hlf_ref, gate_ref,
                    *, sbw, hgk, hgw, d):
    x = x_ref[...]
    ms = jnp.mean(x * x, axis=-1, keepdims=True)
    xn = (x * lax.rsqrt(ms + EPS) * g1_ref[...]).astype(BF16)

    col = [0]

    def proj(width):
        c0 = col[0]
        col[0] = c0 + width
        return _dot(xn, w_ref[:, c0:c0 + width])

    def head_norm(t, g):
        hms = _dot((t * t).astype(BF16), hm_ref[...])
        return t * lax.rsqrt(hms + EPS) * g

    q_ref[...] = head_norm(proj(sbw), qg_ref[...]).astype(BF16)
    even_head = (lax.broadcasted_iota(jnp.int32, (x.shape[0], sbw), 1) // SB_HEAD_DIM) % 2 == 0
    for t, t0_ref, t1_ref in ((head_norm(proj(sbw), kg_ref[...]), k0_ref, k1_ref),
                              (proj(sbw), v0_ref, v1_ref)):
        t0_ref[...] = jnp.where(even_head, t, 0.0).astype(BF16)
        t1_ref[...] = jnp.where(even_head, 0.0, t).astype(BF16)

    hq = proj(hgk)
    hact_ref[:, :hgk] = (hq * _sigmoid(hq)).astype(BF16)

    lbt = lbt_ref[...]
    e = jnp.exp(lbt - jnp.max(lbt, axis=0, keepdims=True))
    lb = e[0:1, :] / jnp.sum(e, axis=0, keepdims=True)
    f = lb + (1.0 - lb) * _sigmoid(proj(hgk))
    hlf_ref[...] = jnp.log(f)
    hact_ref[:, hgk:2 * hgk] = (1.0 - f).astype(BF16)

    hact_ref[:, 2 * hgk:2 * hgk + hgw] = proj(hgw).astype(BF16)
    hg = proj(hgw)
    hact_ref[:, 2 * hgk + hgw:] = (hg * _sigmoid(hg)).astype(BF16)
    gate_ref[:, :d] = _sigmoid(proj(d)).astype(BF16)
    gate_ref[:, d:] = _sigmoid(proj(d)).astype(BF16)


def _in_proj(x2, g1, w_in, qg, kg, lbt, *, sbw, hgk, hgw, tm):
    n, d = x2.shape
    in_w = w_in.shape[1]
    assert in_w == 3 * sbw + 2 * hgk + 2 * hgw + 2 * d
    heads = np.arange(sbw) // SB_HEAD_DIM
    hm = jnp.asarray((heads[:, None] == heads[None, :]) / SB_HEAD_DIM, BF16)

    def rows(width):
        return pl.BlockSpec((tm, width), lambda i: (i, 0))

    def const(shape):
        return pl.BlockSpec(shape, lambda i: (0, 0), pipeline_mode=pl.Buffered(1))

    widths = (sbw, sbw, sbw, sbw, sbw, 2 * hgk + 2 * hgw, hgk, 2 * d)
    dtypes = (BF16, BF16, BF16, BF16, BF16, BF16, F32, BF16)
    out_shapes = tuple(jax.ShapeDtypeStruct((n, w), t) for w, t in zip(widths, dtypes))
    out_specs = tuple(rows(w) for w in widths)
    return pl.pallas_call(
        functools.partial(_in_proj_kernel, sbw=sbw, hgk=hgk, hgw=hgw, d=d),
        grid=(n // tm,),
        in_specs=[rows(d), const((1, d)), const((d, in_w)), const((1, sbw)), const((1, sbw)),
                  const(lbt.shape), const((sbw, sbw))],
        out_specs=out_specs,
        out_shape=out_shapes,
        compiler_params=pltpu.CompilerParams(
            dimension_semantics=("arbitrary",), vmem_limit_bytes=VMEM_LIMIT),
        name="in_proj",
    )(x2, g1, w_in, qg, kg, lbt, hm)


def _sb_query_block(i, q_start, q_ref, k0_ref, k1_ref, v0_ref, v1_ref, t_ref, o_ref,
                    sp_ref, z_ref, u_ref, acc_ref, out_ref, *, tq, tk, group):
    assert tq == 2 * tk
    n_blocks = 2 * i + 2
    row = lax.broadcasted_iota(jnp.int32, (tq, 2 * tk), 0)
    colk = lax.broadcasted_iota(jnp.int32, (tq, 2 * tk), 1) % tk
    pair_lanes = [slice(g * LANES, (g + 1) * LANES) for g in range(group)]

    def causal(diag_offset, rows):
        return (colk + diag_offset < row)[rows]

    def key_start(n):
        return pl.multiple_of((n_blocks - 1 - n) * tk, tk)

    def block_diag(refs, n, psl):
        return jnp.concatenate([r[0, pl.ds(key_start(n), tk), psl] for r in refs], axis=0)


    def logits_mm(g, blk, rows):
        psl = pair_lanes[g]
        start = rows.start or 0
        size = (tq if rows.stop is None else rows.stop) - start
        return _dot_nt(q_ref[0, pl.ds(q_start + start, size), psl],
                       block_diag((k0_ref, k1_ref), blk[0], psl))

    def logits_fin(g, z, blk, rows, diag_offset):
        sp = jnp.maximum(z, 0.0) + jnp.log(1.0 + jnp.exp2(-jnp.abs(z))) * LOG2E
        if diag_offset is not None:
            sp = jnp.where(causal(diag_offset, rows), sp, 0.0)
        sp_ref[blk[1], g, rows] = sp.astype(BF16)
        z_ref[blk[1], g, rows] = z

    def exponent_mm(g, blk, rows):
        return _dot(sp_ref[blk[1], g, rows], t_ref[...])

    def exponent_fin(g, cum, blk, rows, diag_offset):
        total = cum + acc_ref[g, rows]
        u = z_ref[blk[1], g, rows] - total
        if diag_offset is not None:
            u = jnp.where(causal(diag_offset, rows), u, MASKED)
        u_ref[blk[1], g, rows] = u
        half = (total.shape[0], tk)
        acc_ref[g, rows] = jnp.concatenate(
            [jnp.broadcast_to(total[:, 0:1], half), jnp.broadcast_to(total[:, tk:tk + 1], half)], axis=1)

    def weights_mm(g, blk, rows):
        return _dot(jnp.exp2(u_ref[blk[1], g, rows]).astype(BF16),
                    block_diag((v0_ref, v1_ref), blk[0], pair_lanes[g]))

    def weights_fin(g, pv, rows):
        out_ref[g, rows] += pv

    def run(slots):
        units = [(slot, g) for slot in slots for g in range(group)]

        def issue(slot, g):
            return (logits_mm(g, *slot["logits"][:2]) if "logits" in slot else None,
                    exponent_mm(g, *slot["exponent"][:2]) if "exponent" in slot else None,
                    weights_mm(g, *slot["weights"]) if "weights" in slot else None)

        def finish(slot, g, z, cum, pv):
            if "logits" in slot:
                logits_fin(g, z, *slot["logits"])
            if "exponent" in slot:
                exponent_fin(g, cum, *slot["exponent"])
            if "weights" in slot:
                weights_fin(g, pv, slot["weights"][1])

        inflight = issue(*units[0])
        for idx, unit in enumerate(units):
            ready = inflight
            if idx + 1 < len(units):
                inflight = issue(*units[idx + 1])
            finish(*unit, *ready)

    every = slice(None)

    def steady(n, parity):
        return {"logits": ((n, parity), every, None), "exponent": ((n - 1, 1 - parity), every, None),
                "weights": ((n - 2, parity), every)}

    acc_ref[...] = jnp.zeros_like(acc_ref)
    out_ref[...] = jnp.zeros_like(out_ref)

    lower = slice(tk, tq)
    fill = [{"logits": ((0, 0), lower, tk)},
            {"logits": ((1, 1), every, 0), "exponent": ((0, 0), lower, tk)}]
    last = (n_blocks - 1, 1)
    drain = [{"exponent": (last, every, None), "weights": ((n_blocks - 2, 0), every)},
             {"weights": (last, every)}]

    @pl.when(i == 0)
    def _():
        run(fill + [{"exponent": ((1, 1), every, 0), "weights": ((0, 0), lower)},
                    {"weights": ((1, 1), every)}])

    @pl.when(i > 0)
    def _():
        run(fill + [{"logits": ((2, 0), every, None), "exponent": ((1, 1), every, 0), "weights": ((0, 0), lower)},
                    steady(3, 1)])
        slot_pairs = i - 1

        def body(it, carry):
            n = 4 + 8 * it
            run([steady(n + d, d % 2) for d in range(8)])
            return carry

        lax.fori_loop(0, slot_pairs // 4, body, 0)
        done = 4 + 8 * (slot_pairs // 4)

        @pl.when(slot_pairs % 4 >= 2)
        def _():
            run([steady(done + d, d % 2) for d in range(4)])

        @pl.when(slot_pairs % 2 == 1)
        def _():
            run([steady(n_blocks - 2, 0), steady(n_blocks - 1, 1)])

        run(drain)

    for g, psl in enumerate(pair_lanes):
        o_ref[0, pl.ds(q_start, tq), psl] = out_ref[g].astype(BF16)


def _sb_attn_kernel(*refs, tq, tk, group, q_per_step):
    def query_block(j, carry):
        _sb_query_block(pl.program_id(2) * q_per_step + j, pl.multiple_of(j * tq, tq), *refs,
                        tq=tq, tk=tk, group=group)
        return carry

    lax.fori_loop(0, q_per_step, query_block, 0)


def _sb_attention(q, k, v, *, tq, tk, group, q_per_step):
    b, s, w = q.shape
    gw = group * LANES
    assert w % gw == 0
    jj = np.arange(2 * tk)
    tmat = jnp.asarray((jj[:, None] // tk == jj[None, :] // tk) & (jj[:, None] >= jj[None, :]), BF16)

    def keys():
        return pl.BlockSpec((1, s, gw), lambda bi, p, i: (bi, 0, p))

    return pl.pallas_call(
        functools.partial(_sb_attn_kernel, tq=tq, tk=tk, group=group, q_per_step=q_per_step),
        grid=(b, w // gw, s // (q_per_step * tq)),
        in_specs=[pl.BlockSpec((1, q_per_step * tq, gw), lambda bi, p, i: (bi, i, p)),
                  keys(), keys(), keys(), keys(),
                  pl.BlockSpec((2 * tk, 2 * tk), lambda bi, p, i: (0, 0))],
        out_specs=pl.BlockSpec((1, q_per_step * tq, gw), lambda bi, p, i: (bi, i, p)),
        out_shape=jax.ShapeDtypeStruct((b, s, w), BF16),
        scratch_shapes=[pltpu.VMEM((2, group, tq, 2 * tk), BF16),
                        pltpu.VMEM((2, group, tq, 2 * tk), F32),
                        pltpu.VMEM((2, group, tq, 2 * tk), F32),
                        pltpu.VMEM((group, tq, 2 * tk), F32),
                        pltpu.VMEM((group, tq, LANES), F32)],
        compiler_params=pltpu.CompilerParams(
            dimension_semantics=("arbitrary", "arbitrary", "arbitrary"),
            vmem_limit_bytes=VMEM_LIMIT),
        name="sb_attention",
    )(q, *k, *v, tmat)


def _hgrn_tables(c):
    levels = int(np.log2(c))
    assert 2 ** levels == c
    t = np.arange(c)[:, None]
    j = np.arange(c)[None, :]
    blocks = []
    masks = [np.eye(c, dtype=bool)]
    for l in range(1, levels + 1):
        m = 2 ** l
        r = (t // m) * m + m // 2
        upper = (t % m) >= m // 2
        blocks.append(np.where(upper, (j >= r) & (j <= t), (j > t) & (j < r)))
        masks.append((t // m == j // m) & upper & ((j % m) < m // 2))
    blocks.append(j <= t)
    blocks.append(j > t)
    mcat = np.concatenate(blocks, axis=0).astype(np.float32)
    mcat = np.concatenate([mcat, mcat], axis=1)
    masks = np.stack(masks).astype(np.float32)
    masks = np.concatenate([masks, masks], axis=2)
    return jnp.asarray(mcat, BF16), jnp.asarray(masks, F32), levels


def _hgrn_kernel(a_ref, lf_ref, mcat_ref, mask_ref, ng_ref, o_ref,
                 state_ref, *, c, levels, heads, rows):
    @pl.when(pl.program_id(1) == 0)
    def _():
        state_ref[...] = jnp.zeros_like(state_ref)

    dk, dv = HG_KEY_DIM, HG_VAL_DIM
    kw, vw = heads * dk, heads * dv
    zero = jnp.zeros((c, dk), BF16)

    def pair_scores(qs, ks):
        rhs = jnp.concatenate([jnp.concatenate([ks[:, :dk], zero], axis=1),
                               jnp.concatenate([zero, ks[:, dk:]], axis=1)], axis=0)
        return _dot_nt(qs, rhs)

    def prepare(r):
        lf = lf_ref[r]
        hi = lf.astype(BF16)
        lo = (lf - hi.astype(F32)).astype(BF16)
        hilo = jnp.concatenate([hi, lo], axis=0)
        decay = jnp.exp(_dot_tn(hilo, jnp.ones((2 * c, dv), BF16)))
        scores, qbs, kbs = [], [], []
        for p in range(heads // 2):
            psl = slice(2 * p * dk, (2 * p + 2) * dk)
            ex = jnp.exp(_dot(mcat_ref[...], hilo[:, psl]))
            q2 = a_ref[r, :, psl].astype(F32)
            k2 = a_ref[r, :, kw + psl.start:kw + psl.stop].astype(F32)
            sc = [pair_scores(q2.astype(BF16), k2.astype(BF16))]
            for l in range(1, levels + 1):
                e = ex[(l - 1) * c:l * c]
                sc.append(pair_scores((q2 * e).astype(BF16), (k2 * e).astype(BF16)))
            scores.append(sc)
            qbs.append((q2 * ex[levels * c:(levels + 1) * c]).astype(BF16))
            kbs.append((k2 * ex[(levels + 1) * c:(levels + 2) * c]).astype(BF16))
        return decay, scores, qbs, kbs

    def finish(r, decay, scores, qbs, kbs):
        for p in range(heads // 2):
            attn = mask_ref[0] * scores[p][0]
            for l in range(1, levels + 1):
                attn = attn + mask_ref[l] * scores[p][l]
            attn = attn.astype(BF16)
            qb, kb = qbs[p], kbs[p]

            for hh in range(2):
                h = 2 * p + hh
                vsl = slice(h * dv, (h + 1) * dv)
                vh = a_ref[r, :, 2 * kw + vsl.start:2 * kw + vsl.stop]
                st = state_ref[r, h]
                o = _dot(jnp.concatenate([attn[:, hh * c:(hh + 1) * c], qb[:, hh * dk:(hh + 1) * dk]], axis=1),
                         jnp.concatenate([vh, st.astype(BF16)], axis=0))
                state_ref[r, h] = st * decay[h * dk:(h + 1) * dk] + _dot_tn(kb[:, hh * dk:(hh + 1) * dk], vh)

                ms = jnp.mean(o * o, axis=-1, keepdims=True)
                o = o * lax.rsqrt(ms + EPS) * ng_ref[...]
                gate = a_ref[r, :, 2 * kw + vw + vsl.start:2 * kw + vw + vsl.stop]
                o_ref[r, :, vsl] = (o * gate.astype(F32)).astype(BF16)

    prepared = prepare(0)
    for r in range(rows):
        ready = prepared
        if r + 1 < rows:
            prepared = prepare(r + 1)
        finish(r, *ready)


def _hgrn(hact, hlf, ng, *, kw, vw, c, rows):
    b, s, _ = hact.shape
    heads = kw // HG_KEY_DIM
    assert b % rows == 0 and heads % 2 == 0
    mcat, masks, levels = _hgrn_tables(c)

    def seq(width):
        return pl.BlockSpec((rows, c, width), lambda bi, ci: (bi, ci, 0))

    return pl.pallas_call(
        functools.partial(_hgrn_kernel, c=c, levels=levels, heads=heads, rows=rows),
        grid=(b // rows, s // c),
        in_specs=[seq(2 * kw + 2 * vw), seq(kw),
                  pl.BlockSpec(mcat.shape, lambda bi, ci: (0, 0)),
                  pl.BlockSpec(masks.shape, lambda bi, ci: (0, 0, 0)),
                  pl.BlockSpec((1, HG_VAL_DIM), lambda bi, ci: (0, 0))],
        out_specs=seq(vw),
        out_shape=jax.ShapeDtypeStruct((b, s, vw), BF16),
        scratch_shapes=[pltpu.VMEM((rows, heads, HG_KEY_DIM, HG_VAL_DIM), F32)],
        compiler_params=pltpu.CompilerParams(
            dimension_semantics=("arbitrary", "arbitrary"), vmem_limit_bytes=VMEM_LIMIT),
        name="hgrn2",
    )(hact, hlf, mcat, masks, ng)


def _tail_kernel(x_ref, osb_ref, ohg_ref, gate_ref, wb_ref, wo_ref, g2_ref,
                 wg_ref, wu_ref, wd_ref, out_ref, *, sbw, ffn_chunks):
    ysb = _dot(osb_ref[...], wb_ref[:sbw, :])
    yhg = _dot(ohg_ref[...], wb_ref[sbw:, :])
    d = x_ref.shape[1]
    mixed = gate_ref[:, :d].astype(F32) * ysb + gate_ref[:, d:].astype(F32) * yhg
    h = x_ref[...] + _dot(mixed.astype(BF16), wo_ref[...])
    ms = jnp.mean(h * h, axis=-1, keepdims=True)
    hn = (h * lax.rsqrt(ms + EPS) * g2_ref[...]).astype(BF16)
    acc = h
    for c0, c1 in ffn_chunks:
        a = _dot(hn, wg_ref[:, c0:c1])
        u = _dot(hn, wu_ref[:, c0:c1])
        acc = acc + _dot((a * _sigmoid(a) * u).astype(BF16), wd_ref[c0:c1, :])
    out_ref[...] = acc


def _tail(x2, osb, ohg, gates, wb, wo, g2, wg, wu, wd, *, tm):
    n, d = x2.shape
    sbw = osb.shape[1]
    hgw = ohg.shape[1]
    f = wg.shape[1]
    half = (f // 2) // MXU_TILE * MXU_TILE
    ffn_chunks = ((0, half), (half, f)) if 0 < half < f else ((0, f),)

    def rows(width):
        return pl.BlockSpec((tm, width), lambda i: (i, 0))

    def const(shape):
        return pl.BlockSpec(shape, lambda i: (0, 0), pipeline_mode=pl.Buffered(1))

    return pl.pallas_call(
        functools.partial(_tail_kernel, sbw=sbw, ffn_chunks=ffn_chunks),
        grid=(n // tm,),
        in_specs=[rows(d), rows(sbw), rows(hgw), rows(2 * d),
                  const(wb.shape), const(wo.shape), const((1, d)),
                  const(wg.shape), const(wu.shape), const(wd.shape)],
        out_specs=rows(d),
        out_shape=jax.ShapeDtypeStruct((n, d), F32),
        compiler_params=pltpu.CompilerParams(
            dimension_semantics=("arbitrary",), vmem_limit_bytes=VMEM_LIMIT),
        name="tail",
    )(x2, osb, ohg, gates, wb, wo, g2, wg, wu, wd)


def _layer(x, norm1_g, w_in, q_norm_g, k_norm_g, lb_table, hg_norm_g, w_branch, w_out,
           norm2_g, w_gate, w_up, w_down, *, tm_in, tq, tk, group, q_per_step, chunk, hg_rows, tm_tail):
    b, s, d = x.shape
    hgk = lb_table.shape[1]
    mix = w_branch.shape[0]
    in_w = w_in.shape[1]
    sbw = in_w - 2 * hgk - 2 * d - 2 * mix
    hgw = mix - sbw
    assert sbw % LANES == 0 and hgk % HG_KEY_DIM == 0 and hgw % HG_VAL_DIM == 0
    sb_heads = sbw // SB_HEAD_DIM

    x2 = x.reshape(b * s, d)
    scale = SB_HEAD_DIM ** -0.5 * LOG2E
    qg = jnp.tile(q_norm_g.astype(F32) * scale, sb_heads).reshape(1, sbw)
    kg = jnp.tile(k_norm_g.astype(F32), sb_heads).reshape(1, sbw)

    q, k0, k1, v0, v1, hact, hlf, gates = _in_proj(
        x2, norm1_g.reshape(1, d), w_in.astype(BF16), qg, kg, lb_table.astype(F32),
        sbw=sbw, hgk=hgk, hgw=hgw, tm=tm_in)

    def seq(t):
        return t.reshape(b, s, t.shape[-1])

    o_sb = _sb_attention(seq(q), (seq(k0), seq(k1)), (seq(v0), seq(v1)),
                         tq=tq, tk=tk, group=group, q_per_step=q_per_step)
    o_hg = _hgrn(seq(hact), seq(hlf), hg_norm_g.reshape(1, HG_VAL_DIM).astype(F32),
                 kw=hgk, vw=hgw, c=chunk, rows=hg_rows)

    out = _tail(x2, o_sb.reshape(b * s, sbw), o_hg.reshape(b * s, hgw), gates,
                w_branch.astype(BF16), w_out.astype(BF16), norm2_g.reshape(1, d),
                w_gate.astype(BF16), w_up.astype(BF16), w_down.astype(BF16), tm=tm_tail)
    return out.reshape(b, s, d)


def kernel(x, norm1_g, w_in, q_norm_g, k_norm_g, lb_table, hg_norm_g, w_branch, w_out,
           norm2_g, w_ffn_gate, w_ffn_up, w_ffn_down):
    depth = w_in.shape[0]
    assert depth == 1 and lb_table.shape[0] == depth + 1
    return _layer(x, norm1_g[0], w_in[0], q_norm_g[0], k_norm_g[0], lb_table, hg_norm_g[0],
                  w_branch[0], w_out[0], norm2_g[0], w_ffn_gate[0], w_ffn_up[0], w_ffn_down[0],
                  tm_in=1024, tq=256, tk=128, group=4, q_per_step=4,
                  chunk=128, hg_rows=8, tm_tail=512)
```
